```python
import math
import jax, jax.numpy as jnp
from jax import lax
import numpy as np

D_MODEL = 4096
BATCH = 2
SEQ = 4096
DEPTH = 4

N_MIXERS = 3
CONV_WIDTH = 31
RET_HEADS = 16
RET_QK_DIM = D_MODEL // RET_HEADS
RET_V_DIM = 2 * D_MODEL // RET_HEADS
RET_CHUNK = 128
ROPE_BASE = 10000.0
HGRN_HEAD_DIM = 128
HGRN_HEADS = D_MODEL // HGRN_HEAD_DIM
HGRN_CHUNK = 64
MLP_HIDDEN = 4 * D_MODEL
LN_EPS = 1e-5
RMS_EPS = 1e-6
MAX_POS_OFFSET = 1024

kernel_name = "interleaved_conv_retention_hgrn2_deepnorm"


def _layer_norm(u, g, b):
    uf = u.astype(jnp.float32)
    mu = jnp.mean(uf, axis=-1, keepdims=True)
    var = jnp.mean(jnp.square(uf - mu), axis=-1, keepdims=True)
    y = (uf - mu) * lax.rsqrt(var + LN_EPS)
    return (y * g.astype(jnp.float32) + b.astype(jnp.float32)).astype(u.dtype)


def _rms(u):
    uf = u.astype(jnp.float32)
    return uf * lax.rsqrt(jnp.mean(jnp.square(uf), axis=-1, keepdims=True) + RMS_EPS)


def _rotary(u, positions):
    half = u.shape[-1] // 2
    inv_freq = 1.0 / (ROPE_BASE ** jnp.linspace(0.0, 1.0, half, dtype=jnp.float32))
    ang = positions.astype(jnp.float32)[:, :, None, None] * inv_freq
    cos, sin = jnp.cos(ang), jnp.sin(ang)
    u1 = u[..., :half].astype(jnp.float32)
    u2 = u[..., half:].astype(jnp.float32)
    return jnp.concatenate([u1 * cos - u2 * sin, u2 * cos + u1 * sin], axis=-1).astype(u.dtype)


def _to_chunks(u, c):
    b, s = u.shape[:2]
    return u.reshape(b, s // c, c, u.shape[2], u.shape[3]).transpose(1, 0, 3, 2, 4)


def _from_chunks(u):
    n, b, h, c, d = u.shape
    return u.transpose(1, 0, 3, 2, 4).reshape(b, n * c, h, d)


def _conformer_conv(x, w_in, b_in, w_dw, b_dw, ln_g, ln_b, w_out, b_out):
    h = x @ w_in + b_in
    a, gate = jnp.split(h, 2, axis=-1)
    u = a * jax.nn.sigmoid(gate)
    u = lax.conv_general_dilated(
        u, w_dw[:, None, :], window_strides=(1,), padding=[(CONV_WIDTH - 1, 0)],
        dimension_numbers=("NWC", "WIO", "NWC"), feature_group_count=u.shape[-1]) + b_dw
    u = jax.nn.silu(_layer_norm(u, ln_g, ln_b))
    return u @ w_out + b_out


def _retention(x, positions, w_q, w_k, w_v, w_g, w_o):
    b, s, _ = x.shape
    dt = x.dtype
    q = (x @ w_q).reshape(b, s, RET_HEADS, RET_QK_DIM)
    k = (x @ w_k).reshape(b, s, RET_HEADS, RET_QK_DIM) * (RET_QK_DIM ** -0.5)
    v = (x @ w_v).reshape(b, s, RET_HEADS, RET_V_DIM)
    q, k = _rotary(q, positions), _rotary(k, positions)

    log_gamma = jnp.log1p(-(2.0 ** (-5.0 - jnp.arange(RET_HEADS, dtype=jnp.float32))))
    t = jnp.arange(RET_CHUNK, dtype=jnp.float32)
    diff = t[:, None] - t[None, :]
    intra = jnp.where(diff >= 0, jnp.exp(log_gamma[:, None, None] * jnp.maximum(diff, 0.0)), 0.0).astype(dt)
    q_decay = jnp.exp(log_gamma[:, None] * (t + 1.0)).astype(dt)
    k_decay = jnp.exp(log_gamma[:, None] * (RET_CHUNK - 1.0 - t)).astype(dt)
    chunk_decay = jnp.exp(log_gamma * RET_CHUNK).astype(dt)

    qc, kc, vc = _to_chunks(q, RET_CHUNK), _to_chunks(k, RET_CHUNK), _to_chunks(v, RET_CHUNK)

    def step(state, inp):
        qi, ki, vi = inp
        scores = jnp.einsum("bhtd,bhsd->bhts", qi, ki) * intra[None]
        o = jnp.einsum("bhts,bhsv->bhtv", scores, vi) + jnp.einsum(
            "bhtd,bhdv->bhtv", qi * q_decay[None, :, :, None], state)
        state = chunk_decay[None, :, None, None] * state + jnp.einsum(
            "bhsd,bhsv->bhdv", ki * k_decay[None, :, :, None], vi)
        return state, o

    state0 = jnp.zeros((b, RET_HEADS, RET_QK_DIM, RET_V_DIM), dt)
    _, o = lax.scan(step, state0, (qc, kc, vc))
    o = _rms(_from_chunks(o)).astype(dt).reshape(b, s, RET_HEADS * RET_V_DIM)
    o = o * jax.nn.silu(x @ w_g)
    return o @ w_o


def _hgrn2(x, lower_bound, w_q, w_f, w_i, w_g, norm_g, w_o):
    b, s, _ = x.shape
    dt = x.dtype
    q = jax.nn.silu(x @ w_q).reshape(b, s, HGRN_HEADS, HGRN_HEAD_DIM)
    z = (x @ w_f).astype(jnp.float32).reshape(b, s, HGRN_HEADS, HGRN_HEAD_DIM)
    lb = lower_bound.astype(jnp.float32).reshape(HGRN_HEADS, HGRN_HEAD_DIM)
    log_f = jnp.logaddexp(jnp.log(lb), jnp.log1p(-lb) + jax.nn.log_sigmoid(z))
    k = ((1.0 - lb) * jax.nn.sigmoid(-z)).astype(dt)
    i = (x @ w_i).reshape(b, s, HGRN_HEADS, HGRN_HEAD_DIM)

    qc, kc, ic = _to_chunks(q, HGRN_CHUNK), _to_chunks(k, HGRN_CHUNK), _to_chunks(i, HGRN_CHUNK)
    bc = jnp.cumsum(_to_chunks(log_f, HGRN_CHUNK), axis=3)
    causal = jnp.tril(jnp.ones((HGRN_CHUNK, HGRN_CHUNK), dtype=bool))

    def step(state, inp):
        qi, ki, ii, bi = inp
        pair_diff = bi[:, :, :, None, :] - bi[:, :, None, :, :]
        pair = jnp.where(causal[None, None, :, :, None],
                         jnp.exp(jnp.minimum(pair_diff, 0.0)), 0.0).astype(dt)
        scores = jnp.einsum("bhtk,bhsk,bhtsk->bhts", qi, ki, pair)
        b_last = bi[:, :, -1:, :]
        o = jnp.einsum("bhts,bhsv->bhtv", scores, ii) + jnp.einsum(
            "bhtk,bhkv->bhtv", qi * jnp.exp(bi).astype(dt), state)
        state = jnp.exp(b_last[:, :, 0, :]).astype(dt)[..., None] * state + jnp.einsum(
            "bhsk,bhsv->bhkv", ki * jnp.exp(b_last - bi).astype(dt), ii)
        return state, o

    state0 = jnp.zeros((b, HGRN_HEADS, HGRN_HEAD_DIM, HGRN_HEAD_DIM), dt)
    _, o = lax.scan(step, state0, (qc, kc, ic, bc))
    o = _from_chunks(o).reshape(b, s, HGRN_HEADS * HGRN_HEAD_DIM)
    o = (_rms(o) * norm_g.astype(jnp.float32)).astype(dt) * jax.nn.silu(x @ w_g)
    return o @ w_o


def _sq_relu_mlp(x, w1, w2):
    return jnp.square(jax.nn.relu(x @ w1)) @ w2


def setup_inputs(seed: int = 0) -> dict:
    key = jax.random.key(seed)
    ks = iter(jax.random.split(key, 32))
    d = D_MODEL
    n_a = (DEPTH + 2) // 3
    n_b = (DEPTH + 1) // 3
    n_c = DEPTH // 3
    beta = (8.0 * DEPTH) ** -0.25

    def nrm(shape, scale):
        return jax.random.normal(next(ks), shape, jnp.float32) * scale

    x = nrm((BATCH, SEQ, d), 1.0)
    positions = (jax.random.randint(next(ks), (BATCH, 1), 0, MAX_POS_OFFSET)
                 + jnp.arange(SEQ)[None, :]).astype(jnp.int32)
    return {
        "x": x,
        "positions": positions,
        "conv_w_in": nrm((n_a, d, 2 * d), d ** -0.5),
        "conv_b_in": nrm((n_a, 2 * d), 0.02),
        "conv_w_dw": nrm((n_a, CONV_WIDTH, d), CONV_WIDTH ** -0.5),
        "conv_b_dw": nrm((n_a, d), 0.02),
        "conv_ln_g": 1.0 + nrm((n_a, d), 0.02),
        "conv_ln_b": nrm((n_a, d), 0.02),
        "conv_w_out": nrm((n_a, d, d), beta * d ** -0.5),
        "conv_b_out": nrm((n_a, d), 0.02),
        "ret_w_q": nrm((n_b, d, RET_HEADS * RET_QK_DIM), d ** -0.5),
        "ret_w_k": nrm((n_b, d, RET_HEADS * RET_QK_DIM), d ** -0.5),
        "ret_w_v": nrm((n_b, d, RET_HEADS * RET_V_DIM), beta * d ** -0.5),
        "ret_w_g": nrm((n_b, d, RET_HEADS * RET_V_DIM), d ** -0.5),
        "ret_w_o": nrm((n_b, RET_HEADS * RET_V_DIM, d), beta * (RET_HEADS * RET_V_DIM) ** -0.5),
        "hgrn_lower_bounds": nrm((DEPTH, HGRN_HEADS * HGRN_HEAD_DIM), 0.1),
        "hgrn_w_q": nrm((n_c, d, HGRN_HEADS * HGRN_HEAD_DIM), d ** -0.5),
        "hgrn_w_f": nrm((n_c, d, HGRN_HEADS * HGRN_HEAD_DIM), d ** -0.5),
        "hgrn_w_i": nrm((n_c, d, HGRN_HEADS * HGRN_HEAD_DIM), beta * d ** -0.5),
        "hgrn_w_g": nrm((n_c, d, HGRN_HEADS * HGRN_HEAD_DIM), d ** -0.5),
        "hgrn_norm_g": 1.0 + nrm((n_c, HGRN_HEADS * HGRN_HEAD_DIM), 0.02),
        "hgrn_w_o": nrm((n_c, HGRN_HEADS * HGRN_HEAD_DIM, d), beta * d ** -0.5),
        "mlp_w1": nrm((DEPTH, d, MLP_HIDDEN), beta * d ** -0.5),
        "mlp_w2": nrm((DEPTH, MLP_HIDDEN, d), beta * MLP_HIDDEN ** -0.5),
        "ln_mix_g": 1.0 + nrm((DEPTH, d), 0.02),
        "ln_mix_b": nrm((DEPTH, d), 0.02),
        "ln_mlp_g": 1.0 + nrm((DEPTH, d), 0.02),
        "ln_mlp_b": nrm((DEPTH, d), 0.02),
    }


def reference(x, positions, conv_w_in, conv_b_in, conv_w_dw, conv_b_dw, conv_ln_g, conv_ln_b,
              conv_w_out, conv_b_out, ret_w_q, ret_w_k, ret_w_v, ret_w_g, ret_w_o,
              hgrn_lower_bounds, hgrn_w_q, hgrn_w_f, hgrn_w_i, hgrn_w_g, hgrn_norm_g, hgrn_w_o,
              mlp_w1, mlp_w2, ln_mix_g, ln_mix_b, ln_mlp_g, ln_mlp_b):
    alpha = (2.0 * DEPTH) ** 0.25
    lb_all = jnp.cumsum(jax.nn.softmax(hgrn_lower_bounds.astype(jnp.float32), axis=0), axis=0)
    lb_all = lb_all - lb_all[0:1]
    h = x
    for i in range(DEPTH):
        kind, j = i % N_MIXERS, i // N_MIXERS
        if kind == 0:
            y = _conformer_conv(h, conv_w_in[j], conv_b_in[j], conv_w_dw[j], conv_b_dw[j],
                                conv_ln_g[j], conv_ln_b[j], conv_w_out[j], conv_b_out[j])
        elif kind == 1:
            y = _retention(h, positions, ret_w_q[j], ret_w_k[j], ret_w_v[j], ret_w_g[j], ret_w_o[j])
        else:
            y = _hgrn2(h, lb_all[i], hgrn_w_q[j], hgrn_w_f[j], hgrn_w_i[j], hgrn_w_g[j],
                       hgrn_norm_g[j], hgrn_w_o[j])
        h = _layer_norm(alpha * h + y, ln_mix_g[i], ln_mix_b[i])
        h = _layer_norm(alpha * h + _sq_relu_mlp(h, mlp_w1[i], mlp_w2[i]), ln_mlp_g[i], ln_mlp_b[i])
    return h
```

```python
import functools
import math

import jax
import jax.numpy as jnp
from jax import lax
from jax.experimental import pallas as pl
from jax.experimental.pallas import tpu as pltpu

D_MODEL = 4096
DEPTH = 4
N_MIXERS = 3
CONV_WIDTH = 31
RET_HEADS = 16
RET_QK_DIM = D_MODEL // RET_HEADS
RET_V_DIM = 2 * D_MODEL // RET_HEADS
RET_CHUNK = 128
ROPE_BASE = 10000.0
HGRN_HEAD_DIM = 128
HGRN_HEADS = D_MODEL // HGRN_HEAD_DIM
HGRN_CHUNK = 64
LN_EPS = 1e-5
RMS_EPS = 1e-6
ALPHA = (2.0 * DEPTH) ** 0.25

F32 = jnp.float32
BF16 = jnp.bfloat16

VMEM_LIMIT_BYTES = 56 * 1024 * 1024
LANES = 128
SUBLANES = 8
CONV_HALO = 32


def _params(sem):
    return pltpu.CompilerParams(dimension_semantics=sem, vmem_limit_bytes=VMEM_LIMIT_BYTES)


def _sigmoid(x):
    return 1.0 / (1.0 + jnp.exp(-x))


def _mm_body(*refs, n_w, n_extra, n_out, nk, epilogue):
    a_ref = refs[0]
    w_refs = refs[1:1 + n_w]
    extra = refs[1 + n_w:1 + n_w + n_extra]
    outs = refs[1 + n_w + n_extra:1 + n_w + n_extra + n_out]
    accs = refs[1 + n_w + n_extra + n_out:]
    if nk == 1:
        vals = [jnp.dot(a_ref[...], w[...], preferred_element_type=F32) for w in w_refs]
        epilogue(vals, extra, outs)
    else:
        k = pl.program_id(2)

        @pl.when(k == 0)
        def _():
            for acc in accs:
                acc[...] = jnp.zeros_like(acc)

        for acc, w in zip(accs, w_refs):
            acc[...] += jnp.dot(a_ref[...], w[...], preferred_element_type=F32)

        @pl.when(k == nk - 1)
        def _():
            epilogue([acc[...] for acc in accs], extra, outs)


def _matmul(a, w, *, tm, tn, tk, n_out_cols, out_dtypes, epilogue, name,
            w_block_offsets=(0,), extras=()):
    m, kdim = a.shape
    assert m % tm == 0 and kdim % tk == 0 and n_out_cols % tn == 0
    nk = kdim // tk
    grid = (m // tm, n_out_cols // tn, nk)
    in_specs = [pl.BlockSpec((tm, tk), lambda i, j, k: (i, k))]
    operands = [a]
    for off in w_block_offsets:
        in_specs.append(pl.BlockSpec((tk, tn), lambda i, j, k, off=off: (k, j + off)))
        operands.append(w)
    for arr, kind in extras:
        if kind == "row" or kind == "rows":
            in_specs.append(pl.BlockSpec((arr.shape[0], tn), lambda i, j, k: (0, j)))
        elif kind == "tile":
            in_specs.append(pl.BlockSpec((tm, tn), lambda i, j, k: (i, j)))
        elif kind == "side":
            in_specs.append(pl.BlockSpec((tm, arr.shape[1]), lambda i, j, k: (i, 0)))
        else:
            raise ValueError(kind)
        operands.append(arr)
    out_shape = [jax.ShapeDtypeStruct((m, n_out_cols), dt) for dt in out_dtypes]
    out_specs = [pl.BlockSpec((tm, tn), lambda i, j, k: (i, j)) for _ in out_dtypes]
    scratch = []
    if nk > 1:
        scratch = [pltpu.VMEM((tm, tn), F32) for _ in w_block_offsets]
    body = functools.partial(
        _mm_body, n_w=len(w_block_offsets), n_extra=len(extras), n_out=len(out_dtypes),
        nk=nk, epilogue=epilogue)
    res = pl.pallas_call(
        body, grid=grid, in_specs=in_specs, out_specs=out_specs, out_shape=out_shape,
        scratch_shapes=scratch, name=name,
        compiler_params=_params(("parallel", "parallel", "arbitrary")),
    )(*operands)
    return res


def _ep_plain(vals, extra, outs):
    outs[0][...] = vals[0].astype(outs[0].dtype)


def _ep_silu(vals, extra, outs):
    y = vals[0]
    outs[0][...] = (y * _sigmoid(y)).astype(outs[0].dtype)


def _ep_relu2(vals, extra, outs):
    y = jnp.maximum(vals[0], 0.0)
    outs[0][...] = (y * y).astype(outs[0].dtype)


def _ep_glu(vals, extra, outs):
    a = vals[0] + extra[0][...]
    g = vals[1] + extra[1][...]
    outs[0][...] = (a * _sigmoid(g)).astype(outs[0].dtype)


def _ep_residual(vals, extra, outs):
    y = vals[0]
    if len(extra) > 1:
        y = y + extra[1][...]
    outs[0][...] = ALPHA * extra[0][...] + y


def _ep_rotary(vals, extra, outs, *, scale):
    cos = extra[0][...]
    sin = extra[1][...]
    half = RET_QK_DIM // 2
    y = vals[0]
    tn = y.shape[1]
    for h in range(tn // RET_QK_DIM):
        lo = h * RET_QK_DIM
        u1 = y[:, lo:lo + half] * scale
        u2 = y[:, lo + half:lo + 2 * half] * scale
        outs[0][:, lo:lo + half] = u1 * cos - u2 * sin
        outs[0][:, lo + half:lo + 2 * half] = u2 * cos + u1 * sin


def _ep_hgrn_forget(vals, extra, outs, *, layer):
    p = extra[0][...]
    p = p - jnp.max(p, axis=0, keepdims=True)
    e = jnp.exp(p)
    sm = e / jnp.sum(e, axis=0, keepdims=True)
    lb = jnp.sum(sm[1:layer + 1, :], axis=0, keepdims=True)
    z = vals[0]
    log_sig = jnp.minimum(z, 0.0) - jnp.log1p(jnp.exp(-jnp.abs(z)))
    a = jnp.log(lb)
    c = jnp.log1p(-lb) + log_sig
    log_f = jnp.maximum(a, c) + jnp.log1p(jnp.exp(-jnp.abs(a - c)))
    outs[0][...] = log_f
    outs[1][...] = (1.0 - lb) * _sigmoid(-z)


def _ln_rows(u, g, b):
    mu = jnp.mean(u, axis=-1, keepdims=True)
    d = u - mu
    var = jnp.mean(d * d, axis=-1, keepdims=True)
    return d * lax.rsqrt(var + LN_EPS) * g + b


def _ln_body(z_ref, g_ref, b_ref, o32_ref, o16_ref):
    y = _ln_rows(z_ref[...], g_ref[...], b_ref[...])
    o32_ref[...] = y
    o16_ref[...] = y.astype(BF16)


def _layer_norm(z, g, b, *, tm=256):
    m, d = z.shape
    row = pl.BlockSpec((tm, d), lambda i: (i, 0))
    vec = pl.BlockSpec((1, d), lambda i: (0, 0))
    return pl.pallas_call(
        _ln_body, grid=(m // tm,), in_specs=[row, vec, vec], out_specs=[row, row],
        out_shape=[jax.ShapeDtypeStruct((m, d), F32), jax.ShapeDtypeStruct((m, d), BF16)],
        name="layer_norm", compiler_params=_params(("parallel",)),
    )(z, g.reshape(1, d), b.reshape(1, d))


CONV_ROW_TILE = 64


def _dwconv_body(u_ref, halo_ref, w_ref, bdw_ref, g_ref, b_ref, o_ref, xbuf, cbuf, *, ts):
    s = pl.program_id(1)
    d = u_ref.shape[2]
    xbuf[0:CONV_HALO, :] = jnp.where(s > 0, halo_ref[0], 0.0)
    xbuf[CONV_HALO:, :] = u_ref[0]
    first = CONV_HALO - (CONV_WIDTH - 1)

    def col_body(c, carry):
        c0 = pl.multiple_of(c * LANES, LANES)
        w = w_ref[:, pl.ds(c0, LANES)]
        bias = bdw_ref[:, pl.ds(c0, LANES)]
        for r in range(ts // CONV_ROW_TILE):
            r0 = r * CONV_ROW_TILE
            acc = jnp.broadcast_to(bias, (CONV_ROW_TILE, LANES))
            for j in range(CONV_WIDTH):
                acc = acc + w[j:j + 1, :] * xbuf[pl.ds(r0 + first + j, CONV_ROW_TILE), pl.ds(c0, LANES)]
            cbuf[pl.ds(r0, CONV_ROW_TILE), pl.ds(c0, LANES)] = acc
        return carry

    lax.fori_loop(0, d // LANES, col_body, 0)
    y = _ln_rows(cbuf[...], g_ref[...], b_ref[...])
    o_ref[0] = (y * _sigmoid(y)).astype(o_ref.dtype)


def _dwconv_ln_silu(u, w_dw, b_dw, ln_g, ln_b, *, ts=256):
    bsz, s, d = u.shape
    per = ts // CONV_HALO
    vec = pl.BlockSpec((1, d), lambda b, i: (0, 0))
    return pl.pallas_call(
        functools.partial(_dwconv_body, ts=ts),
        grid=(bsz, s // ts),
        in_specs=[
            pl.BlockSpec((1, ts, d), lambda b, i: (b, i, 0)),
            pl.BlockSpec((1, CONV_HALO, d), lambda b, i: (b, jnp.maximum(i * per - 1, 0), 0)),
            pl.BlockSpec((CONV_WIDTH, d), lambda b, i: (0, 0)),
            vec, vec, vec,
        ],
        out_specs=pl.BlockSpec((1, ts, d), lambda b, i: (b, i, 0)),
        out_shape=jax.ShapeDtypeStruct((bsz, s, d), BF16),
        scratch_shapes=[pltpu.VMEM((ts + CONV_HALO, d), F32), pltpu.VMEM((ts, d), F32)],
        name="dwconv_ln_silu", compiler_params=_params(("parallel", "arbitrary")),
    )(u, u, w_dw, b_dw.reshape(1, d), ln_g.reshape(1, d), ln_b.reshape(1, d))


def _rope_body(pos_ref, cos_ref, sin_ref):
    half = cos_ref.shape[1]
    lane = lax.broadcasted_iota(jnp.int32, (1, half), 1).astype(F32)
    inv_freq = jnp.exp(lane * (-math.log(ROPE_BASE) / (half - 1)))
    ang = pos_ref[...].astype(F32) * inv_freq
    cos_ref[...] = jnp.cos(ang)
    sin_ref[...] = jnp.sin(ang)


def _rope_tables(positions, *, tm=1024):
    m = positions.size
    half = RET_QK_DIM // 2
    out = pl.BlockSpec((tm, half), lambda i: (i, 0))
    return pl.pallas_call(
        _rope_body, grid=(m // tm,),
        in_specs=[pl.BlockSpec((tm, 1), lambda i: (i, 0))],
        out_specs=[out, out],
        out_shape=[jax.ShapeDtypeStruct((m, half), F32)] * 2,
        name="rope_tables", compiler_params=_params(("parallel",)),
    )(positions.reshape(m, 1))


def _ret_body(q_ref, k_ref, v_ref, g_ref, o_ref, state_ref, *, rows):
    h = pl.program_id(1)
    n = pl.program_id(2)
    c = RET_CHUNK

    @pl.when(n == 0)
    def _():
        state_ref[...] = jnp.zeros_like(state_ref)

    hf = jnp.full((1, 1), h, jnp.int32).astype(F32)
    log_gamma = jnp.log1p(-jnp.exp2(-5.0 - hf))
    t_col = lax.broadcasted_iota(jnp.int32, (c, 1), 0).astype(F32)
    t_row = lax.broadcasted_iota(jnp.int32, (1, c), 1).astype(F32)
    diff = t_col - t_row
    intra = jnp.where(diff >= 0, jnp.exp(log_gamma * jnp.maximum(diff, 0.0)), 0.0)
    q_decay = jnp.exp(log_gamma * (t_col + 1.0))
    k_decay = jnp.exp(log_gamma * (c - 1.0 - t_col))
    chunk_decay = jnp.exp(log_gamma * c)

    for ci in range(rows // c):
        sl = pl.ds(ci * c, c)
        qi = q_ref[0, sl, :]
        ki = k_ref[0, sl, :]
        vi = v_ref[0, sl, :].astype(BF16)
        state = state_ref[...]
        scores = lax.dot_general(qi.astype(BF16), ki.astype(BF16), (((1,), (1,)), ((), ())),
                                 preferred_element_type=F32) * intra
        o = jnp.dot(scores.astype(BF16), vi, preferred_element_type=F32)
        o = o + jnp.dot((qi * q_decay).astype(BF16), state.astype(BF16), preferred_element_type=F32)
        state_ref[...] = chunk_decay * state + lax.dot_general(
            (ki * k_decay).astype(BF16), vi, (((0,), (0,)), ((), ())), preferred_element_type=F32)
        o = o * lax.rsqrt(jnp.mean(o * o, axis=-1, keepdims=True) + RMS_EPS)
        o_ref[0, sl, :] = (o * g_ref[0, sl, :]).astype(o_ref.dtype)


def _retention_core(q, k, v, g, *, rows=512):
    bsz, s, _ = q.shape
    qk = pl.BlockSpec((1, rows, RET_QK_DIM), lambda b, h, n: (b, n, h))
    vv = pl.BlockSpec((1, rows, RET_V_DIM), lambda b, h, n: (b, n, h))
    return pl.pallas_call(
        functools.partial(_ret_body, rows=rows),
        grid=(bsz, RET_HEADS, s // rows),
        in_specs=[qk, qk, vv, vv], out_specs=vv,
        out_shape=jax.ShapeDtypeStruct((bsz, s, RET_HEADS * RET_V_DIM), BF16),
        scratch_shapes=[pltpu.VMEM((RET_QK_DIM, RET_V_DIM), F32)],
        name="retention_core", compiler_params=_params(("parallel", "parallel", "arbitrary")),
    )(q, k, v, g)


def _hgrn_chunk(q, kk, lf, ii, state_t):
    c = HGRN_CHUNK
    row = lax.broadcasted_iota(jnp.int32, (c, HGRN_HEAD_DIM), 0)
    b = lf
    shift = 1
    while shift < c:
        b = b + jnp.where(row >= shift, pltpu.roll(b, shift, 0), 0.0)
        shift *= 2

    lane_t = lax.broadcasted_iota(jnp.int32, (c, c), 1)
    row_s = lax.broadcasted_iota(jnp.int32, (c, c), 0)
    scores_t = jnp.zeros((c, c), F32)
    for tb in range(c // SUBLANES):
        n = (tb + 1) * SUBLANES
        b_s = b[:n]
        k_s = kk[:n]
        for tt in range(SUBLANES):
            t = tb * SUBLANES + tt
            w = (k_s * q[t:t + 1, :]) * jnp.exp(jnp.minimum(b[t:t + 1, :] - b_s, 0.0))
            col = jnp.sum(w, axis=-1, keepdims=True)
            if n < c:
                col = jnp.concatenate([col, jnp.zeros((c - n, 1), F32)], axis=0)
            scores_t = jnp.where(lane_t == t, col, scores_t)
    scores_t = jnp.where(row_s <= lane_t, scores_t, 0.0)

    i16 = ii.astype(BF16)
    o = lax.dot_general(scores_t.astype(BF16), i16, (((0,), (0,)), ((), ())),
                        preferred_element_type=F32)
    o = o + lax.dot_general((q * jnp.exp(b)).astype(BF16), state_t.astype(BF16),
                            (((1,), (1,)), ((), ())), preferred_element_type=F32)
    b_last = b[c - 1:c, :]
    kd = (kk * jnp.exp(b_last - b)).astype(BF16)
    new_state_t = jnp.exp(b_last) * state_t + lax.dot_general(
        i16, kd, (((0,), (0,)), ((), ())), preferred_element_type=F32)
    return o, new_state_t


def _hgrn_body(q_ref, k_ref, lf_ref, i_ref, o_ref, state_ref, *, rows):
    n = pl.program_id(2)

    @pl.when(n == 0)
    def _():
        state_ref[...] = jnp.zeros_like(state_ref)

    def chunk_body(ci, carry):
        sl = pl.ds(pl.multiple_of(ci * HGRN_CHUNK, HGRN_CHUNK), HGRN_CHUNK)
        o, new_state = _hgrn_chunk(q_ref[0, sl, :], k_ref[0, sl, :], lf_ref[0, sl, :],
                                   i_ref[0, sl, :], state_ref[...])
        state_ref[...] = new_state
        o_ref[0, sl, :] = o
        return carry

    lax.fori_loop(0, rows // HGRN_CHUNK, chunk_body, 0)


def _hgrn_core(q, k, lf, ii, *, rows=512):
    bsz, s, _ = q.shape
    blk = pl.BlockSpec((1, rows, HGRN_HEAD_DIM), lambda b, h, n: (b, n, h))
    return pl.pallas_call(
        functools.partial(_hgrn_body, rows=rows),
        grid=(bsz, HGRN_HEADS, s // rows),
        in_specs=[blk, blk, blk, blk], out_specs=blk,
        out_shape=jax.ShapeDtypeStruct((bsz, s, HGRN_HEADS * HGRN_HEAD_DIM), F32),
        scratch_shapes=[pltpu.VMEM((HGRN_HEAD_DIM, HGRN_HEAD_DIM), F32)],
        name="hgrn_core", compiler_params=_params(("parallel", "parallel", "arbitrary")),
    )(q, k, lf, ii)


def _hgrn_gate_body(o_ref, g_ref, ng_ref, out_ref):
    o = o_ref[...]
    o = o * lax.rsqrt(jnp.mean(o * o, axis=-1, keepdims=True) + RMS_EPS)
    out_ref[...] = ((o * ng_ref[...]) * g_ref[...]).astype(out_ref.dtype)


def _hgrn_gate(o, g, norm_g, *, tm=256):
    m, d = o.shape
    row = pl.BlockSpec((tm, d), lambda i: (i, 0))
    return pl.pallas_call(
        _hgrn_gate_body, grid=(m // tm,),
        in_specs=[row, row, pl.BlockSpec((1, d), lambda i: (0, 0))], out_specs=row,
        out_shape=jax.ShapeDtypeStruct((m, d), BF16),
        name="hgrn_gate", compiler_params=_params(("parallel",)),
    )(o, g, norm_g.reshape(1, d))


TM = 1024
TN = 1024
TN_NARROW = 512


def _conformer_layer(h32, h16, bsz, w_in, b_in, w_dw, b_dw, ln_g, ln_b, w_out, b_out):
    m, d = h32.shape
    b_in = b_in.reshape(1, 2 * d)
    u = _matmul(h16, w_in.astype(BF16), tm=TM, tn=TN_NARROW, tk=d, n_out_cols=d,
                out_dtypes=[F32], epilogue=_ep_glu, name="conv_in_glu",
                w_block_offsets=(0, d // TN_NARROW),
                extras=[(b_in[:, :d], "row"), (b_in[:, d:], "row")])[0]
    u = _dwconv_ln_silu(u.reshape(bsz, m // bsz, d), w_dw, b_dw, ln_g, ln_b).reshape(m, d)
    return _matmul(u, w_out.astype(BF16), tm=TM, tn=TN_NARROW, tk=d, n_out_cols=d,
                   out_dtypes=[F32], epilogue=_ep_residual, name="conv_out_res",
                   extras=[(h32, "tile"), (b_out.reshape(1, d), "row")])[0]


def _retention_layer(h32, h16, bsz, cos, sin, w_q, w_k, w_v, w_g, w_o):
    m, d = h32.shape
    s = m // bsz
    rot = [(cos, "side"), (sin, "side")]
    q = _matmul(h16, w_q.astype(BF16), tm=TM, tn=TN, tk=d, n_out_cols=d, out_dtypes=[F32],
                epilogue=functools.partial(_ep_rotary, scale=1.0), name="ret_q", extras=rot)[0]
    k = _matmul(h16, w_k.astype(BF16), tm=TM, tn=TN, tk=d, n_out_cols=d, out_dtypes=[F32],
                epilogue=functools.partial(_ep_rotary, scale=RET_QK_DIM ** -0.5), name="ret_k",
                extras=rot)[0]
    dv = RET_HEADS * RET_V_DIM
    v = _matmul(h16, w_v.astype(BF16), tm=TM, tn=TN, tk=d, n_out_cols=dv, out_dtypes=[F32],
                epilogue=_ep_plain, name="ret_v")[0]
    g = _matmul(h16, w_g.astype(BF16), tm=TM, tn=TN, tk=d, n_out_cols=dv, out_dtypes=[F32],
                epilogue=_ep_silu, name="ret_g")[0]
    o = _retention_core(q.reshape(bsz, s, d), k.reshape(bsz, s, d), v.reshape(bsz, s, dv),
                        g.reshape(bsz, s, dv)).reshape(m, dv)
    return _matmul(o, w_o.astype(BF16), tm=TM, tn=TN_NARROW, tk=d, n_out_cols=d,
                   out_dtypes=[F32], epilogue=_ep_residual, name="ret_out_res",
                   extras=[(h32, "tile")])[0]


def _hgrn_layer(h32, h16, bsz, layer, lower_bounds, w_q, w_f, w_i, w_g, norm_g, w_o):
    m, d = h32.shape
    s = m // bsz
    q = _matmul(h16, w_q.astype(BF16), tm=TM, tn=TN, tk=d, n_out_cols=d, out_dtypes=[F32],
                epilogue=_ep_silu, name="hgrn_q")[0]
    log_f, k = _matmul(h16, w_f.astype(BF16), tm=TM, tn=TN_NARROW, tk=d, n_out_cols=d,
                       out_dtypes=[F32, F32],
                       epilogue=functools.partial(_ep_hgrn_forget, layer=layer), name="hgrn_f",
                       extras=[(lower_bounds, "rows")])
    ii = _matmul(h16, w_i.astype(BF16), tm=TM, tn=TN, tk=d, n_out_cols=d, out_dtypes=[F32],
                 epilogue=_ep_plain, name="hgrn_i")[0]
    g = _matmul(h16, w_g.astype(BF16), tm=TM, tn=TN, tk=d, n_out_cols=d, out_dtypes=[F32],
                epilogue=_ep_silu, name="hgrn_g")[0]
    shp = (bsz, s, d)
    o = _hgrn_core(q.reshape(shp), k.reshape(shp), log_f.reshape(shp), ii.reshape(shp)).reshape(m, d)
    o = _hgrn_gate(o, g, norm_g)
    return _matmul(o, w_o.astype(BF16), tm=TM, tn=TN_NARROW, tk=d, n_out_cols=d,
                   out_dtypes=[F32], epilogue=_ep_residual, name="hgrn_out_res",
                   extras=[(h32, "tile")])[0]


def _mlp_layer(h32, h16, w1, w2):
    m, d = h32.shape
    hidden = w1.shape[1]
    a = _matmul(h16, w1.astype(BF16), tm=TM, tn=TN, tk=d, n_out_cols=hidden, out_dtypes=[BF16],
                epilogue=_ep_relu2, name="mlp_up")[0]
    return _matmul(a, w2.astype(BF16), tm=TM, tn=TN, tk=2048, n_out_cols=d, out_dtypes=[F32],
                   epilogue=_ep_residual, name="mlp_down_res", extras=[(h32, "tile")])[0]


def kernel(x, positions, conv_w_in, conv_b_in, conv_w_dw, conv_b_dw, conv_ln_g, conv_ln_b,
           conv_w_out, conv_b_out, ret_w_q, ret_w_k, ret_w_v, ret_w_g, ret_w_o,
           hgrn_lower_bounds, hgrn_w_q, hgrn_w_f, hgrn_w_i, hgrn_w_g, hgrn_norm_g, hgrn_w_o,
           mlp_w1, mlp_w2, ln_mix_g, ln_mix_b, ln_mlp_g, ln_mlp_b):
    bsz, s, d = x.shape
    m = bsz * s
    h32 = x.reshape(m, d)
    h16 = h32.astype(BF16)
    cos, sin = _rope_tables(positions)
    for i in range(DEPTH):
        kind, j = i % N_MIXERS, i // N_MIXERS
        if kind == 0:
            z = _conformer_layer(h32, h16, bsz, conv_w_in[j], conv_b_in[j], conv_w_dw[j],
                                 conv_b_dw[j], conv_ln_g[j], conv_ln_b[j], conv_w_out[j],
                                 conv_b_out[j])
        elif kind == 1:
            z = _retention_layer(h32, h16, bsz, cos, sin, ret_w_q[j], ret_w_k[j], ret_w_v[j],
                                 ret_w_g[j], ret_w_o[j])
        else:
            z = _hgrn_layer(h32, h16, bsz, i, hgrn_lower_bounds, hgrn_w_q[j], hgrn_w_f[j],
                            hgrn_w_i[j], hgrn_w_g[j], hgrn_norm_g[j], hgrn_w_o[j])
        h32, h16 = _layer_norm(z, ln_mix_g[i], ln_mix_b[i])
        z = _mlp_layer(h32, h16, mlp_w1[i], mlp_w2[i])
        h32, h16 = _layer_norm(z, ln_mlp_g[i], ln_mlp_b[i])
    return h32.reshape(bsz, s, d)
```

```python
import functools
import math

import jax
import jax.numpy as jnp
from jax import lax
from jax.experimental import pallas as pl
from jax.experimental.pallas import tpu as pltpu

D_MODEL = 4096
DEPTH = 4
N_MIXERS = 3
CONV_WIDTH = 31
RET_HEADS = 16
RET_QK_DIM = D_MODEL // RET_HEADS
RET_V_DIM = 2 * D_MODEL // RET_HEADS
RET_CHUNK = 128
ROPE_BASE = 10000.0
HGRN_HEAD_DIM = 128
HGRN_HEADS = D_MODEL // HGRN_HEAD_DIM
HGRN_CHUNK = 64
LN_EPS = 1e-5
RMS_EPS = 1e-6
ALPHA = (2.0 * DEPTH) ** 0.25

F32 = jnp.float32
BF16 = jnp.bfloat16

VMEM_LIMIT_BYTES = 56 * 1024 * 1024
LANES = 128
SUBLANES = 8
CONV_HALO = 32


def _params(sem):
    return pltpu.CompilerParams(dimension_semantics=sem, vmem_limit_bytes=VMEM_LIMIT_BYTES)


def _sigmoid(x):
    return 1.0 / (1.0 + jnp.exp(-x))


def _mm_body(*refs, n_w, n_extra, n_out, nk, epilogue):
    a_ref = refs[0]
    w_refs = refs[1:1 + n_w]
    extra = refs[1 + n_w:1 + n_w + n_extra]
    outs = refs[1 + n_w + n_extra:1 + n_w + n_extra + n_out]
    accs = refs[1 + n_w + n_extra + n_out:]
    if nk == 1:
        vals = [jnp.dot(a_ref[...], w[...].astype(BF16), preferred_element_type=F32)
                for w in w_refs]
        epilogue(vals, extra, outs)
    else:
        k = pl.program_id(2)

        @pl.when(k == 0)
        def _():
            for acc in accs:
                acc[...] = jnp.zeros_like(acc)

        for acc, w in zip(accs, w_refs):
            acc[...] += jnp.dot(a_ref[...], w[...].astype(BF16), preferred_element_type=F32)

        @pl.when(k == nk - 1)
        def _():
            epilogue([acc[...] for acc in accs], extra, outs)


def _matmul(a, w, layer, *, tm, tn, tk, n_out_cols, out_dtypes, epilogue, name,
            w_block_offsets=(0,), extras=()):
    m, kdim = a.shape
    assert m % tm == 0 and kdim % tk == 0 and n_out_cols % tn == 0
    nk = kdim // tk
    grid = (m // tm, n_out_cols // tn, nk)
    in_specs = [pl.BlockSpec((tm, tk), lambda i, j, k: (i, k))]
    operands = [a]
    for off in w_block_offsets:
        in_specs.append(pl.BlockSpec((None, tk, tn), lambda i, j, k, off=off: (layer, k, j + off)))
        operands.append(w)
    for arr, kind in extras:
        if kind == "row" or kind == "rows":
            in_specs.append(pl.BlockSpec((arr.shape[0], tn), lambda i, j, k: (0, j)))
        elif kind == "tile":
            in_specs.append(pl.BlockSpec((tm, tn), lambda i, j, k: (i, j)))
        elif kind == "side":
            in_specs.append(pl.BlockSpec((tm, arr.shape[1]), lambda i, j, k: (i, 0)))
        else:
            raise ValueError(kind)
        operands.append(arr)
    out_shape = [jax.ShapeDtypeStruct((m, n_out_cols), dt) for dt in out_dtypes]
    out_specs = [pl.BlockSpec((tm, tn), lambda i, j, k: (i, j)) for _ in out_dtypes]
    scratch = []
    if nk > 1:
        scratch = [pltpu.VMEM((tm, tn), F32) for _ in w_block_offsets]
    body = functools.partial(
        _mm_body, n_w=len(w_block_offsets), n_extra=len(extras), n_out=len(out_dtypes),
        nk=nk, epilogue=epilogue)
    res = pl.pallas_call(
        body, grid=grid, in_specs=in_specs, out_specs=out_specs, out_shape=out_shape,
        scratch_shapes=scratch, name=name,
        compiler_params=_params(("parallel", "parallel", "arbitrary")),
    )(*operands)
    return res


def _ep_plain(vals, extra, outs):
    outs[0][...] = vals[0].astype(outs[0].dtype)


def _ep_silu(vals, extra, outs):
    y = vals[0]
    outs[0][...] = (y * _sigmoid(y)).astype(outs[0].dtype)


def _ep_relu2(vals, extra, outs):
    y = jnp.maximum(vals[0], 0.0)
    outs[0][...] = (y * y).astype(outs[0].dtype)


def _ep_glu(vals, extra, outs):
    a = vals[0] + extra[0][...]
    g = vals[1] + extra[1][...]
    outs[0][...] = (a * _sigmoid(g)).astype(outs[0].dtype)


def _ep_residual(vals, extra, outs):
    y = vals[0]
    if len(extra) > 1:
        y = y + extra[1][...]
    outs[0][...] = ALPHA * extra[0][...] + y


def _ep_rotary(vals, extra, outs, *, scale):
    cos = extra[0][...]
    sin = extra[1][...]
    half = RET_QK_DIM // 2
    y = vals[0]
    tn = y.shape[1]
    for h in range(tn // RET_QK_DIM):
        lo = h * RET_QK_DIM
        u1 = y[:, lo:lo + half] * scale
        u2 = y[:, lo + half:lo + 2 * half] * scale
        outs[0][:, lo:lo + half] = u1 * cos - u2 * sin
        outs[0][:, lo + half:lo + 2 * half] = u2 * cos + u1 * sin


def _ep_hgrn_forget(vals, extra, outs, *, layer):
    p = extra[0][...]
    p = p - jnp.max(p, axis=0, keepdims=True)
    e = jnp.exp(p)
    sm = e / jnp.sum(e, axis=0, keepdims=True)
    lb = jnp.sum(sm[1:layer + 1, :], axis=0, keepdims=True)
    z = vals[0]
    log_sig = jnp.minimum(z, 0.0) - jnp.log1p(jnp.exp(-jnp.abs(z)))
    a = jnp.log(lb)
    c = jnp.log1p(-lb) + log_sig
    log_f = jnp.maximum(a, c) + jnp.log1p(jnp.exp(-jnp.abs(a - c)))
    outs[0][...] = log_f
    outs[1][...] = (1.0 - lb) * _sigmoid(-z)


def _ln_rows(u, g, b):
    mu = jnp.mean(u, axis=-1, keepdims=True)
    d = u - mu
    var = jnp.mean(d * d, axis=-1, keepdims=True)
    return d * lax.rsqrt(var + LN_EPS) * g + b


def _ln_body(z_ref, g_ref, b_ref, o32_ref, o16_ref):
    y = _ln_rows(z_ref[...], g_ref[...], b_ref[...])
    o32_ref[...] = y
    o16_ref[...] = y.astype(BF16)


def _layer_norm(z, g, b, *, tm=256):
    m, d = z.shape
    row = pl.BlockSpec((tm, d), lambda i: (i, 0))
    vec = pl.BlockSpec((1, d), lambda i: (0, 0))
    return pl.pallas_call(
        _ln_body, grid=(m // tm,), in_specs=[row, vec, vec], out_specs=[row, row],
        out_shape=[jax.ShapeDtypeStruct((m, d), F32), jax.ShapeDtypeStruct((m, d), BF16)],
        name="layer_norm", compiler_params=_params(("parallel",)),
    )(z, g.reshape(1, d), b.reshape(1, d))


CONV_ROW_TILE = 64


def _dwconv_body(u_ref, halo_ref, w_ref, bdw_ref, g_ref, b_ref, o_ref, xbuf, cbuf, *, ts):
    s = pl.program_id(1)
    d = u_ref.shape[2]
    xbuf[0:CONV_HALO, :] = jnp.where(s > 0, halo_ref[0], 0.0)
    xbuf[CONV_HALO:, :] = u_ref[0]
    first = CONV_HALO - (CONV_WIDTH - 1)

    def col_body(c, carry):
        c0 = pl.multiple_of(c * LANES, LANES)
        w = w_ref[:, pl.ds(c0, LANES)]
        bias = bdw_ref[:, pl.ds(c0, LANES)]
        for r in range(ts // CONV_ROW_TILE):
            r0 = r * CONV_ROW_TILE
            acc = jnp.broadcast_to(bias, (CONV_ROW_TILE, LANES))
            for j in range(CONV_WIDTH):
                acc = acc + w[j:j + 1, :] * xbuf[pl.ds(r0 + first + j, CONV_ROW_TILE), pl.ds(c0, LANES)]
            cbuf[pl.ds(r0, CONV_ROW_TILE), pl.ds(c0, LANES)] = acc
        return carry

    lax.fori_loop(0, d // LANES, col_body, 0)
    y = _ln_rows(cbuf[...], g_ref[...], b_ref[...])
    o_ref[0] = (y * _sigmoid(y)).astype(o_ref.dtype)


def _dwconv_ln_silu(u, w_dw, b_dw, ln_g, ln_b, *, ts=256):
    bsz, s, d = u.shape
    per = ts // CONV_HALO
    vec = pl.BlockSpec((1, d), lambda b, i: (0, 0))
    return pl.pallas_call(
        functools.partial(_dwconv_body, ts=ts),
        grid=(bsz, s // ts),
        in_specs=[
            pl.BlockSpec((1, ts, d), lambda b, i: (b, i, 0)),
            pl.BlockSpec((1, CONV_HALO, d), lambda b, i: (b, jnp.maximum(i * per - 1, 0), 0)),
            pl.BlockSpec((CONV_WIDTH, d), lambda b, i: (0, 0)),
            vec, vec, vec,
        ],
        out_specs=pl.BlockSpec((1, ts, d), lambda b, i: (b, i, 0)),
        out_shape=jax.ShapeDtypeStruct((bsz, s, d), BF16),
        scratch_shapes=[pltpu.VMEM((ts + CONV_HALO, d), F32), pltpu.VMEM((ts, d), F32)],
        name="dwconv_ln_silu", compiler_params=_params(("parallel", "arbitrary")),
    )(u, u, w_dw, b_dw.reshape(1, d), ln_g.reshape(1, d), ln_b.reshape(1, d))


def _rope_body(pos_ref, cos_ref, sin_ref):
    half = cos_ref.shape[1]
    lane = lax.broadcasted_iota(jnp.int32, (1, half), 1).astype(F32)
    inv_freq = jnp.exp(lane * (-math.log(ROPE_BASE) / (half - 1)))
    ang = pos_ref[...].astype(F32) * inv_freq
    cos_ref[...] = jnp.cos(ang)
    sin_ref[...] = jnp.sin(ang)


def _rope_tables(positions, *, tm=1024):
    m = positions.size
    half = RET_QK_DIM // 2
    out = pl.BlockSpec((tm, half), lambda i: (i, 0))
    return pl.pallas_call(
        _rope_body, grid=(m // tm,),
        in_specs=[pl.BlockSpec((tm, 1), lambda i: (i, 0))],
        out_specs=[out, out],
        out_shape=[jax.ShapeDtypeStruct((m, half), F32)] * 2,
        name="rope_tables", compiler_params=_params(("parallel",)),
    )(positions.reshape(m, 1))


def _ret_body(q_ref, k_ref, v_ref, g_ref, o_ref, state_ref, *, rows):
    h = pl.program_id(1)
    n = pl.program_id(2)
    c = RET_CHUNK

    @pl.when(n == 0)
    def _():
        state_ref[...] = jnp.zeros_like(state_ref)

    hf = jnp.full((1, 1), h, jnp.int32).astype(F32)
    log_gamma = jnp.log1p(-jnp.exp2(-5.0 - hf))
    t_col = lax.broadcasted_iota(jnp.int32, (c, 1), 0).astype(F32)
    t_row = lax.broadcasted_iota(jnp.int32, (1, c), 1).astype(F32)
    diff = t_col - t_row
    intra = jnp.where(diff >= 0, jnp.exp(log_gamma * jnp.maximum(diff, 0.0)), 0.0)
    q_decay = jnp.exp(log_gamma * (t_col + 1.0))
    k_decay = jnp.exp(log_gamma * (c - 1.0 - t_col))
    chunk_decay = jnp.exp(log_gamma * c)

    for ci in range(rows // c):
        sl = pl.ds(ci * c, c)
        qi = q_ref[0, sl, :]
        ki = k_ref[0, sl, :]
        vi = v_ref[0, sl, :].astype(BF16)
        state = state_ref[...]
        scores = lax.dot_general(qi.astype(BF16), ki.astype(BF16), (((1,), (1,)), ((), ())),
                                 preferred_element_type=F32) * intra
        o = jnp.dot(scores.astype(BF16), vi, preferred_element_type=F32)
        o = o + jnp.dot((qi * q_decay).astype(BF16), state.astype(BF16), preferred_element_type=F32)
        state_ref[...] = chunk_decay * state + lax.dot_general(
            (ki * k_decay).astype(BF16), vi, (((0,), (0,)), ((), ())), preferred_element_type=F32)
        o = o * lax.rsqrt(jnp.mean(o * o, axis=-1, keepdims=True) + RMS_EPS)
        o_ref[0, sl, :] = (o * g_ref[0, sl, :]).astype(o_ref.dtype)


def _retention_core(q, k, v, g, *, rows=512):
    bsz, s, _ = q.shape
    qk = pl.BlockSpec((1, rows, RET_QK_DIM), lambda b, h, n: (b, n, h))
    vv = pl.BlockSpec((1, rows, RET_V_DIM), lambda b, h, n: (b, n, h))
    return pl.pallas_call(
        functools.partial(_ret_body, rows=rows),
        grid=(bsz, RET_HEADS, s // rows),
        in_specs=[qk, qk, vv, vv], out_specs=vv,
        out_shape=jax.ShapeDtypeStruct((bsz, s, RET_HEADS * RET_V_DIM), BF16),
        scratch_shapes=[pltpu.VMEM((RET_QK_DIM, RET_V_DIM), F32)],
        name="retention_core", compiler_params=_params(("parallel", "parallel", "arbitrary")),
    )(q, k, v, g)


def _hgrn_chunk(q, kk, lf, ii, state_t):
    c = HGRN_CHUNK
    row = lax.broadcasted_iota(jnp.int32, (c, HGRN_HEAD_DIM), 0)
    b = lf
    shift = 1
    while shift < c:
        b = b + jnp.where(row >= shift, pltpu.roll(b, shift, 0), 0.0)
        shift *= 2

    sub = SUBLANES
    nb = c // sub
    blk = row // sub
    firsts = [b[i * sub:i * sub + 1, :] for i in range(nb)]
    r_full = jnp.concatenate([jnp.broadcast_to(r, (sub, HGRN_HEAD_DIM)) for r in firsts], axis=0)
    qr = q * jnp.exp(jnp.minimum(b - r_full, 0.0))
    k_cols, q_cols = [], []
    for i in range(1, nb):
        n = i * sub
        ki = kk[:n] * jnp.exp(jnp.minimum(firsts[i] - b[:n], 0.0))
        ki = jnp.concatenate([ki, jnp.zeros((c - n, HGRN_HEAD_DIM), F32)], axis=0)
        k_cols.append(ki.astype(BF16))
        q_cols.append(jnp.where(blk == i, qr, 0.0).astype(BF16))
    off_t = lax.dot_general(jnp.concatenate(k_cols, axis=1), jnp.concatenate(q_cols, axis=1),
                            (((1,), (1,)), ((), ())), preferred_element_type=F32)

    lane_t8 = lax.broadcasted_iota(jnp.int32, (sub, c), 1)
    d_blocks = []
    for i in range(nb):
        b_i = b[i * sub:(i + 1) * sub]
        k_i = kk[i * sub:(i + 1) * sub]
        d = jnp.zeros((sub, c), F32)
        for tt in range(sub):
            t = i * sub + tt
            w = (k_i * q[t:t + 1, :]) * jnp.exp(jnp.minimum(b[t:t + 1, :] - b_i, 0.0))
            d = jnp.where(lane_t8 == t, jnp.sum(w, axis=-1, keepdims=True), d)
        d_blocks.append(d)
    lane_t = lax.broadcasted_iota(jnp.int32, (c, c), 1)
    row_s = lax.broadcasted_iota(jnp.int32, (c, c), 0)
    scores_t = jnp.where(row_s <= lane_t, jnp.concatenate(d_blocks, axis=0), 0.0) + off_t

    i16 = ii.astype(BF16)
    o = lax.dot_general(scores_t.astype(BF16), i16, (((0,), (0,)), ((), ())),
                        preferred_element_type=F32)
    o = o + lax.dot_general((q * jnp.exp(b)).astype(BF16), state_t.astype(BF16),
                            (((1,), (1,)), ((), ())), preferred_element_type=F32)
    b_last = b[c - 1:c, :]
    kd = (kk * jnp.exp(b_last - b)).astype(BF16)
    new_state_t = jnp.exp(b_last) * state_t + lax.dot_general(
        i16, kd, (((0,), (0,)), ((), ())), preferred_element_type=F32)
    return o, new_state_t


HGRN_HEADS_PER_STEP = 4


def _hgrn_body(q_ref, k_ref, lf_ref, i_ref, o_ref, state_ref, *, rows):
    n = pl.program_id(2)

    @pl.when(n == 0)
    def _():
        state_ref[...] = jnp.zeros_like(state_ref)

    def chunk_body(ci, carry):
        sl = pl.ds(pl.multiple_of(ci * HGRN_CHUNK, HGRN_CHUNK), HGRN_CHUNK)
        for hh in range(HGRN_HEADS_PER_STEP):
            cols = pl.ds(hh * HGRN_HEAD_DIM, HGRN_HEAD_DIM)
            o, new_state = _hgrn_chunk(q_ref[0, sl, cols], k_ref[0, sl, cols], lf_ref[0, sl, cols],
                                       i_ref[0, sl, cols], state_ref[hh])
            state_ref[hh] = new_state
            o_ref[0, sl, cols] = o
        return carry

    lax.fori_loop(0, rows // HGRN_CHUNK, chunk_body, 0)


def _hgrn_core(q, k, lf, ii, *, rows=512):
    bsz, s, _ = q.shape
    width = HGRN_HEADS_PER_STEP * HGRN_HEAD_DIM
    blk = pl.BlockSpec((1, rows, width), lambda b, h, n: (b, n, h))
    return pl.pallas_call(
        functools.partial(_hgrn_body, rows=rows),
        grid=(bsz, HGRN_HEADS // HGRN_HEADS_PER_STEP, s // rows),
        in_specs=[blk, blk, blk, blk], out_specs=blk,
        out_shape=jax.ShapeDtypeStruct((bsz, s, HGRN_HEADS * HGRN_HEAD_DIM), F32),
        scratch_shapes=[pltpu.VMEM((HGRN_HEADS_PER_STEP, HGRN_HEAD_DIM, HGRN_HEAD_DIM), F32)],
        name="hgrn_core", compiler_params=_params(("parallel", "parallel", "arbitrary")),
    )(q, k, lf, ii)


def _hgrn_gate_body(o_ref, g_ref, ng_ref, out_ref):
    o = o_ref[...]
    o = o * lax.rsqrt(jnp.mean(o * o, axis=-1, keepdims=True) + RMS_EPS)
    out_ref[...] = ((o * ng_ref[...]) * g_ref[...]).astype(out_ref.dtype)


def _hgrn_gate(o, g, norm_g, *, tm=256):
    m, d = o.shape
    row = pl.BlockSpec((tm, d), lambda i: (i, 0))
    return pl.pallas_call(
        _hgrn_gate_body, grid=(m // tm,),
        in_specs=[row, row, pl.BlockSpec((1, d), lambda i: (0, 0))], out_specs=row,
        out_shape=jax.ShapeDtypeStruct((m, d), BF16),
        name="hgrn_gate", compiler_params=_params(("parallel",)),
    )(o, g, norm_g.reshape(1, d))


TM = 1024
TN = 1024
TN_NARROW = 512


def _conformer_layer(h32, h16, bsz, j, w_in, b_in, w_dw, b_dw, ln_g, ln_b, w_out, b_out):
    m, d = h32.shape
    b_in = b_in.reshape(1, 2 * d)
    tn_glu = TN_NARROW // 2
    u = _matmul(h16, w_in, j, tm=TM, tn=tn_glu, tk=d, n_out_cols=d,
                out_dtypes=[F32], epilogue=_ep_glu, name="conv_in_glu",
                w_block_offsets=(0, d // tn_glu),
                extras=[(b_in[:, :d], "row"), (b_in[:, d:], "row")])[0]
    u = _dwconv_ln_silu(u.reshape(bsz, m // bsz, d), w_dw, b_dw, ln_g, ln_b).reshape(m, d)
    return _matmul(u, w_out, j, tm=TM, tn=TN_NARROW, tk=d, n_out_cols=d,
                   out_dtypes=[F32], epilogue=_ep_residual, name="conv_out_res",
                   extras=[(h32, "tile"), (b_out.reshape(1, d), "row")])[0]


def _retention_layer(h32, h16, bsz, j, cos, sin, w_q, w_k, w_v, w_g, w_o):
    m, d = h32.shape
    s = m // bsz
    rot = [(cos, "side"), (sin, "side")]
    q = _matmul(h16, w_q, j, tm=TM, tn=TN_NARROW, tk=d, n_out_cols=d, out_dtypes=[F32],
                epilogue=functools.partial(_ep_rotary, scale=1.0), name="ret_q", extras=rot)[0]
    k = _matmul(h16, w_k, j, tm=TM, tn=TN_NARROW, tk=d, n_out_cols=d, out_dtypes=[F32],
                epilogue=functools.partial(_ep_rotary, scale=RET_QK_DIM ** -0.5), name="ret_k",
                extras=rot)[0]
    dv = RET_HEADS * RET_V_DIM
    v = _matmul(h16, w_v, j, tm=TM, tn=TN_NARROW, tk=d, n_out_cols=dv, out_dtypes=[F32],
                epilogue=_ep_plain, name="ret_v")[0]
    g = _matmul(h16, w_g, j, tm=TM, tn=TN_NARROW, tk=d, n_out_cols=dv, out_dtypes=[F32],
                epilogue=_ep_silu, name="ret_g")[0]
    o = _retention_core(q.reshape(bsz, s, d), k.reshape(bsz, s, d), v.reshape(bsz, s, dv),
                        g.reshape(bsz, s, dv)).reshape(m, dv)
    return _matmul(o, w_o, j, tm=TM, tn=TN, tk=2048, n_out_cols=d,
                   out_dtypes=[F32], epilogue=_ep_residual, name="ret_out_res",
                   extras=[(h32, "tile")])[0]


def _hgrn_layer(h32, h16, bsz, layer, j, lower_bounds, w_q, w_f, w_i, w_g, norm_g, w_o):
    m, d = h32.shape
    s = m // bsz
    q = _matmul(h16, w_q, j, tm=TM, tn=TN_NARROW, tk=d, n_out_cols=d, out_dtypes=[F32],
                epilogue=_ep_silu, name="hgrn_q")[0]
    log_f, k = _matmul(h16, w_f, j, tm=TM, tn=TN_NARROW, tk=d, n_out_cols=d,
                       out_dtypes=[F32, F32],
                       epilogue=functools.partial(_ep_hgrn_forget, layer=layer), name="hgrn_f",
                       extras=[(lower_bounds, "rows")])
    ii = _matmul(h16, w_i, j, tm=TM, tn=TN_NARROW, tk=d, n_out_cols=d, out_dtypes=[F32],
                 epilogue=_ep_plain, name="hgrn_i")[0]
    g = _matmul(h16, w_g, j, tm=TM, tn=TN_NARROW, tk=d, n_out_cols=d, out_dtypes=[F32],
                epilogue=_ep_silu, name="hgrn_g")[0]
    shp = (bsz, s, d)
    o = _hgrn_core(q.reshape(shp), k.reshape(shp), log_f.reshape(shp), ii.reshape(shp)).reshape(m, d)
    o = _hgrn_gate(o, g, norm_g)
    return _matmul(o, w_o, j, tm=TM, tn=TN_NARROW, tk=d, n_out_cols=d,
                   out_dtypes=[F32], epilogue=_ep_residual, name="hgrn_out_res",
                   extras=[(h32, "tile")])[0]


def _mlp_layer(h32, h16, i, w1, w2):
    m, d = h32.shape
    hidden = w1.shape[2]
    a = _matmul(h16, w1, i, tm=TM, tn=TN_NARROW, tk=d, n_out_cols=hidden, out_dtypes=[BF16],
                epilogue=_ep_relu2, name="mlp_up")[0]
    return _matmul(a, w2, i, tm=TM, tn=TN, tk=2048, n_out_cols=d, out_dtypes=[F32],
                   epilogue=_ep_residual, name="mlp_down_res", extras=[(h32, "tile")])[0]


def kernel(x, positions, conv_w_in, conv_b_in, conv_w_dw, conv_b_dw, conv_ln_g, conv_ln_b,
           conv_w_out, conv_b_out, ret_w_q, ret_w_k, ret_w_v, ret_w_g, ret_w_o,
           hgrn_lower_bounds, hgrn_w_q, hgrn_w_f, hgrn_w_i, hgrn_w_g, hgrn_norm_g, hgrn_w_o,
           mlp_w1, mlp_w2, ln_mix_g, ln_mix_b, ln_mlp_g, ln_mlp_b):
    bsz, s, d = x.shape
    m = bsz * s
    h32 = x.reshape(m, d)
    h16 = h32.astype(BF16)
    cos, sin = _rope_tables(positions)
    for i in range(DEPTH):
        kind, j = i % N_MIXERS, i // N_MIXERS
        if kind == 0:
            z = _conformer_layer(h32, h16, bsz, j, conv_w_in, conv_b_in[j], conv_w_dw[j],
                                 conv_b_dw[j], conv_ln_g[j], conv_ln_b[j], conv_w_out,
                                 conv_b_out[j])
        elif kind == 1:
            z = _retention_layer(h32, h16, bsz, j, cos, sin, ret_w_q, ret_w_k, ret_w_v,
                                 ret_w_g, ret_w_o)
        else:
            z = _hgrn_layer(h32, h16, bsz, i, j, hgrn_lower_bounds, hgrn_w_q, hgrn_w_f,
                            hgrn_w_i, hgrn_w_g, hgrn_norm_g[j], hgrn_w_o)
        h32, h16 = _layer_norm(z, ln_mix_g[i], ln_mix_b[i])
        z = _mlp_layer(h32, h16, i, mlp_w1, mlp_w2)
        h32, h16 = _layer_norm(z, ln_mlp_g[i], ln_mlp_b[i])
    return h32.reshape(bsz, s, d)
```

```python
import functools
import math

import jax
import jax.numpy as jnp
from jax import lax
from jax.experimental import pallas as pl
from jax.experimental.pallas import tpu as pltpu

D_MODEL = 4096
DEPTH = 4
N_MIXERS = 3
CONV_WIDTH = 31
RET_HEADS = 16
RET_QK_DIM = D_MODEL // RET_HEADS
RET_V_DIM = 2 * D_MODEL // RET_HEADS
RET_CHUNK = 128
ROPE_BASE = 10000.0
HGRN_HEAD_DIM = 128
HGRN_HEADS = D_MODEL // HGRN_HEAD_DIM
HGRN_CHUNK = 64
LN_EPS = 1e-5
RMS_EPS = 1e-6
ALPHA = (2.0 * DEPTH) ** 0.25

F32 = jnp.float32
BF16 = jnp.bfloat16

VMEM_LIMIT_BYTES = 56 * 1024 * 1024
LANES = 128
SUBLANES = 8
CONV_HALO = 32


def _params(sem):
    return pltpu.CompilerParams(dimension_semantics=sem, vmem_limit_bytes=VMEM_LIMIT_BYTES)


def _sigmoid(x):
    return 1.0 / (1.0 + jnp.exp(-x))


def _mm_body(*refs, n_w, n_extra, n_out, nk, epilogue):
    a_ref = refs[0]
    w_refs = refs[1:1 + n_w]
    extra = refs[1 + n_w:1 + n_w + n_extra]
    outs = refs[1 + n_w + n_extra:1 + n_w + n_extra + n_out]
    accs = refs[1 + n_w + n_extra + n_out:]
    if nk == 1:
        vals = [jnp.dot(a_ref[...], w[...].astype(BF16), preferred_element_type=F32)
                for w in w_refs]
        epilogue(vals, extra, outs)
    else:
        k = pl.program_id(2)

        @pl.when(k == 0)
        def _():
            for acc in accs:
                acc[...] = jnp.zeros_like(acc)

        for acc, w in zip(accs, w_refs):
            acc[...] += jnp.dot(a_ref[...], w[...].astype(BF16), preferred_element_type=F32)

        @pl.when(k == nk - 1)
        def _():
            epilogue([acc[...] for acc in accs], extra, outs)


def _matmul(a, w, layer, *, tm, tn, tk, n_out_cols, out_dtypes, epilogue, name,
            w_block_offsets=(0,), extras=()):
    m, kdim = a.shape
    assert m % tm == 0 and kdim % tk == 0 and n_out_cols % tn == 0
    nk = kdim // tk
    grid = (m // tm, n_out_cols // tn, nk)
    in_specs = [pl.BlockSpec((tm, tk), lambda i, j, k: (i, k))]
    operands = [a]
    for off in w_block_offsets:
        in_specs.append(pl.BlockSpec((None, tk, tn), lambda i, j, k, off=off: (layer, k, j + off)))
        operands.append(w)
    for arr, kind in extras:
        if kind == "row" or kind == "rows":
            in_specs.append(pl.BlockSpec((arr.shape[0], tn), lambda i, j, k: (0, j)))
        elif kind == "tile":
            in_specs.append(pl.BlockSpec((tm, tn), lambda i, j, k: (i, j)))
        elif kind == "side":
            in_specs.append(pl.BlockSpec((tm, arr.shape[1]), lambda i, j, k: (i, 0)))
        else:
            raise ValueError(kind)
        operands.append(arr)
    out_shape = [jax.ShapeDtypeStruct((m, n_out_cols), dt) for dt in out_dtypes]
    out_specs = [pl.BlockSpec((tm, tn), lambda i, j, k: (i, j)) for _ in out_dtypes]
    scratch = []
    if nk > 1:
        scratch = [pltpu.VMEM((tm, tn), F32) for _ in w_block_offsets]
    body = functools.partial(
        _mm_body, n_w=len(w_block_offsets), n_extra=len(extras), n_out=len(out_dtypes),
        nk=nk, epilogue=epilogue)
    res = pl.pallas_call(
        body, grid=grid, in_specs=in_specs, out_specs=out_specs, out_shape=out_shape,
        scratch_shapes=scratch, name=name,
        compiler_params=_params(("parallel", "parallel", "arbitrary")),
    )(*operands)
    return res


def _ep_plain(vals, extra, outs):
    outs[0][...] = vals[0].astype(outs[0].dtype)


def _ep_silu(vals, extra, outs):
    y = vals[0]
    outs[0][...] = (y * _sigmoid(y)).astype(outs[0].dtype)


def _ep_relu2(vals, extra, outs):
    y = jnp.maximum(vals[0], 0.0)
    outs[0][...] = (y * y).astype(outs[0].dtype)


def _ep_glu(vals, extra, outs):
    a = vals[0] + extra[0][...]
    g = vals[1] + extra[1][...]
    outs[0][...] = (a * _sigmoid(g)).astype(outs[0].dtype)


def _ep_residual(vals, extra, outs):
    y = vals[0]
    if len(extra) > 1:
        y = y + extra[1][...]
    outs[0][...] = ALPHA * extra[0][...] + y


def _ep_rotary(vals, extra, outs, *, scale):
    cos = extra[0][...]
    sin = extra[1][...]
    half = RET_QK_DIM // 2
    y = vals[0]
    tn = y.shape[1]
    for h in range(tn // RET_QK_DIM):
        lo = h * RET_QK_DIM
        u1 = y[:, lo:lo + half] * scale
        u2 = y[:, lo + half:lo + 2 * half] * scale
        outs[0][:, lo:lo + half] = u1 * cos - u2 * sin
        outs[0][:, lo + half:lo + 2 * half] = u2 * cos + u1 * sin


def _ep_hgrn_forget(vals, extra, outs, *, layer):
    p = extra[0][...]
    p = p - jnp.max(p, axis=0, keepdims=True)
    e = jnp.exp(p)
    sm = e / jnp.sum(e, axis=0, keepdims=True)
    lb = jnp.sum(sm[1:layer + 1, :], axis=0, keepdims=True)
    z = vals[0]
    log_sig = jnp.minimum(z, 0.0) - jnp.log1p(jnp.exp(-jnp.abs(z)))
    a = jnp.log(lb)
    c = jnp.log1p(-lb) + log_sig
    log_f = jnp.maximum(a, c) + jnp.log1p(jnp.exp(-jnp.abs(a - c)))
    outs[0][...] = log_f
    outs[1][...] = (1.0 - lb) * _sigmoid(-z)


def _ln_rows(u, g, b):
    mu = jnp.mean(u, axis=-1, keepdims=True)
    d = u - mu
    var = jnp.mean(d * d, axis=-1, keepdims=True)
    return d * lax.rsqrt(var + LN_EPS) * g + b


def _ln_body(*refs, with_residual):
    if with_residual:
        z_ref, h_ref, g_ref, b_ref, o32_ref, o16_ref = refs
        u = ALPHA * h_ref[...] + z_ref[...]
    else:
        z_ref, g_ref, b_ref, o32_ref, o16_ref = refs
        u = z_ref[...]
    y = _ln_rows(u, g_ref[...], b_ref[...])
    o32_ref[...] = y
    o16_ref[...] = y.astype(BF16)


def _layer_norm(z, g, b, residual=None, *, tm=256):
    m, d = z.shape
    row = pl.BlockSpec((tm, d), lambda i: (i, 0))
    vec = pl.BlockSpec((1, d), lambda i: (0, 0))
    acts = [z] if residual is None else [z, residual]
    return pl.pallas_call(
        functools.partial(_ln_body, with_residual=residual is not None),
        grid=(m // tm,), in_specs=[row] * len(acts) + [vec, vec], out_specs=[row, row],
        out_shape=[jax.ShapeDtypeStruct((m, d), F32), jax.ShapeDtypeStruct((m, d), BF16)],
        name="layer_norm", compiler_params=_params(("parallel",)),
    )(*acts, g.reshape(1, d), b.reshape(1, d))


CONV_ROW_TILE = 64
CONV_LN_ROWS = 16


def _dwconv_body(u_ref, halo_ref, w_ref, bdw_ref, g_ref, b_ref, o_ref, cbuf, *, ts):
    s = pl.program_id(1)
    d = u_ref.shape[2]
    window = CONV_HALO + CONV_ROW_TILE
    first = CONV_HALO - (CONV_WIDTH - 1)

    def col_body(c, carry):
        cols = pl.ds(pl.multiple_of(c * LANES, LANES), LANES)
        w = w_ref[:, cols]
        bias = bdw_ref[:, cols]
        for r in range(ts // CONV_ROW_TILE):
            r0 = r * CONV_ROW_TILE
            if r == 0:
                halo = jnp.where(s > 0, halo_ref[0, :, cols], 0.0)
                win = jnp.concatenate([halo, u_ref[0, 0:CONV_ROW_TILE, cols]], axis=0)
            else:
                win = u_ref[0, r0 - CONV_HALO:r0 + CONV_ROW_TILE, cols]
            acc = jnp.broadcast_to(bias, (CONV_ROW_TILE, LANES))
            for sh in range(SUBLANES):
                shifted = win if sh == 0 else pltpu.roll(win, window - sh, 0)
                for j in range(CONV_WIDTH):
                    if (first + j) % SUBLANES == sh:
                        a = (first + j) - sh
                        acc = acc + w[j:j + 1, :] * shifted[a:a + CONV_ROW_TILE]
            cbuf[r0:r0 + CONV_ROW_TILE, cols] = acc
        return carry

    lax.fori_loop(0, d // LANES, col_body, 0)

    def ln_body(i, carry):
        rows = pl.ds(pl.multiple_of(i * CONV_LN_ROWS, CONV_LN_ROWS), CONV_LN_ROWS)
        y = _ln_rows(cbuf[rows, :], g_ref[...], b_ref[...])
        o_ref[0, rows, :] = (y * _sigmoid(y)).astype(o_ref.dtype)
        return carry

    lax.fori_loop(0, ts // CONV_LN_ROWS, ln_body, 0, unroll=2)


def _dwconv_ln_silu(u, w_dw, b_dw, ln_g, ln_b, *, ts=256):
    bsz, s, d = u.shape
    per = ts // CONV_HALO
    vec = pl.BlockSpec((1, d), lambda b, i: (0, 0))
    return pl.pallas_call(
        functools.partial(_dwconv_body, ts=ts),
        grid=(bsz, s // ts),
        in_specs=[
            pl.BlockSpec((1, ts, d), lambda b, i: (b, i, 0)),
            pl.BlockSpec((1, CONV_HALO, d), lambda b, i: (b, jnp.maximum(i * per - 1, 0), 0)),
            pl.BlockSpec((CONV_WIDTH, d), lambda b, i: (0, 0)),
            vec, vec, vec,
        ],
        out_specs=pl.BlockSpec((1, ts, d), lambda b, i: (b, i, 0)),
        out_shape=jax.ShapeDtypeStruct((bsz, s, d), BF16),
        scratch_shapes=[pltpu.VMEM((ts, d), F32)],
        name="dwconv_ln_silu", compiler_params=_params(("parallel", "arbitrary")),
    )(u, u, w_dw, b_dw.reshape(1, d), ln_g.reshape(1, d), ln_b.reshape(1, d))


def _rope_body(pos_ref, cos_ref, sin_ref):
    half = cos_ref.shape[1]
    lane = lax.broadcasted_iota(jnp.int32, (1, half), 1).astype(F32)
    inv_freq = jnp.exp(lane * (-math.log(ROPE_BASE) / (half - 1)))
    ang = pos_ref[...].astype(F32) * inv_freq
    cos_ref[...] = jnp.cos(ang)
    sin_ref[...] = jnp.sin(ang)


def _rope_tables(positions, *, tm=1024):
    m = positions.size
    half = RET_QK_DIM // 2
    out = pl.BlockSpec((tm, half), lambda i: (i, 0))
    return pl.pallas_call(
        _rope_body, grid=(m // tm,),
        in_specs=[pl.BlockSpec((tm, 1), lambda i: (i, 0))],
        out_specs=[out, out],
        out_shape=[jax.ShapeDtypeStruct((m, half), F32)] * 2,
        name="rope_tables", compiler_params=_params(("parallel",)),
    )(positions.reshape(m, 1))


def _ret_body(q_ref, k_ref, v_ref, g_ref, o_ref, state_ref, *, rows):
    h = pl.program_id(1)
    n = pl.program_id(2)
    c = RET_CHUNK

    @pl.when(n == 0)
    def _():
        state_ref[...] = jnp.zeros_like(state_ref)

    hf = jnp.full((1, 1), h, jnp.int32).astype(F32)
    log_gamma = jnp.log1p(-jnp.exp2(-5.0 - hf))
    t_col = lax.broadcasted_iota(jnp.int32, (c, 1), 0).astype(F32)
    t_row = lax.broadcasted_iota(jnp.int32, (1, c), 1).astype(F32)
    diff = t_col - t_row
    intra = jnp.where(diff >= 0, jnp.exp(log_gamma * jnp.maximum(diff, 0.0)), 0.0)
    q_decay = jnp.exp(log_gamma * (t_col + 1.0))
    k_decay = jnp.exp(log_gamma * (c - 1.0 - t_col))
    chunk_decay = jnp.exp(log_gamma * c)

    for ci in range(rows // c):
        sl = pl.ds(ci * c, c)
        qi = q_ref[0, sl, :]
        ki = k_ref[0, sl, :]
        vi = v_ref[0, sl, :].astype(BF16)
        state = state_ref[...]
        scores = lax.dot_general(qi.astype(BF16), ki.astype(BF16), (((1,), (1,)), ((), ())),
                                 preferred_element_type=F32) * intra
        o = jnp.dot(scores.astype(BF16), vi, preferred_element_type=F32)
        o = o + jnp.dot((qi * q_decay).astype(BF16), state.astype(BF16), preferred_element_type=F32)
        state_ref[...] = chunk_decay * state + lax.dot_general(
            (ki * k_decay).astype(BF16), vi, (((0,), (0,)), ((), ())), preferred_element_type=F32)
        o = o * lax.rsqrt(jnp.mean(o * o, axis=-1, keepdims=True) + RMS_EPS)
        o_ref[0, sl, :] = (o * g_ref[0, sl, :]).astype(o_ref.dtype)


def _retention_core(q, k, v, g, *, rows=1024):
    bsz, s, _ = q.shape
    qk = pl.BlockSpec((1, rows, RET_QK_DIM), lambda b, h, n: (b, n, h))
    vv = pl.BlockSpec((1, rows, RET_V_DIM), lambda b, h, n: (b, n, h))
    return pl.pallas_call(
        functools.partial(_ret_body, rows=rows),
        grid=(bsz, RET_HEADS, s // rows),
        in_specs=[qk, qk, vv, vv], out_specs=vv,
        out_shape=jax.ShapeDtypeStruct((bsz, s, RET_HEADS * RET_V_DIM), BF16),
        scratch_shapes=[pltpu.VMEM((RET_QK_DIM, RET_V_DIM), F32)],
        name="retention_core", compiler_params=_params(("parallel", "parallel", "arbitrary")),
    )(q, k, v, g)


LOG2E = math.log2(math.e)


def _hgrn_chunks(qs, kks, lfs, iis, states_t):
    c = HGRN_CHUNK
    sub = SUBLANES
    nb = c // sub
    heads = range(len(qs))
    dk = HGRN_HEAD_DIM

    row8 = lax.broadcasted_iota(jnp.int32, (sub, dk), 0)
    scan_masks = [(sh, row8 >= sh) for sh in (1, 2, 4)]
    b2s = []
    for h in heads:
        blocks, total = [], None
        for i in range(nb):
            x = lfs[h][i * sub:(i + 1) * sub] * LOG2E
            for sh, mask in scan_masks:
                x = x + jnp.where(mask, pltpu.roll(x, sh, 0), 0.0)
            if total is not None:
                x = x + total
            total = jnp.broadcast_to(x[sub - 1:sub, :], (sub, dk))
            blocks.append(x)
        b2s.append(blocks)

    outs = [_hgrn_scores_t(qs[h], kks[h], b2s[h]) for h in heads]
    lane_t8 = lax.broadcasted_iota(jnp.int32, (sub, c), 1)
    diag = [[jnp.zeros((sub, c), F32) for _ in range(nb)] for _ in heads]
    for i in range(nb):
        for tt in range(sub):
            t = i * sub + tt
            hit = lane_t8 == t
            for h in heads:
                b_i = b2s[h][i]
                k_i = kks[h][i * sub:(i + 1) * sub]
                w = (k_i * qs[h][t:t + 1, :]) * jnp.exp2(jnp.minimum(b_i[tt:tt + 1, :] - b_i, 0.0))
                diag[h][i] = jnp.where(hit, jnp.sum(w, axis=-1, keepdims=True), diag[h][i])
    lane_t = lax.broadcasted_iota(jnp.int32, (c, c), 1)
    row_s = lax.broadcasted_iota(jnp.int32, (c, c), 0)
    causal = row_s <= lane_t

    results = []
    for h in heads:
        q, kk, ii, state_t = qs[h], kks[h], iis[h], states_t[h]
        b2 = jnp.concatenate(b2s[h], axis=0)
        scores_t = jnp.where(causal, jnp.concatenate(diag[h], axis=0), 0.0) + outs[h]
        i16 = ii.astype(BF16)
        o = lax.dot_general(scores_t.astype(BF16), i16, (((0,), (0,)), ((), ())),
                            preferred_element_type=F32)
        o = o + lax.dot_general((q * jnp.exp2(b2)).astype(BF16), state_t.astype(BF16),
                                (((1,), (1,)), ((), ())), preferred_element_type=F32)
        b_last = b2[c - 1:c, :]
        kd = (kk * jnp.exp2(b_last - b2)).astype(BF16)
        new_state_t = jnp.exp2(b_last) * state_t + lax.dot_general(
            i16, kd, (((0,), (0,)), ((), ())), preferred_element_type=F32)
        results.append((o, new_state_t))
    return results


def _hgrn_scores_t(q, kk, b2_blocks):
    c = HGRN_CHUNK
    sub = SUBLANES
    nb = c // sub
    row = lax.broadcasted_iota(jnp.int32, (c, HGRN_HEAD_DIM), 0)
    b = jnp.concatenate(b2_blocks, axis=0)
    blk = row // sub
    firsts = [blk_b[0:1, :] for blk_b in b2_blocks]
    r_full = jnp.concatenate([jnp.broadcast_to(r, (sub, HGRN_HEAD_DIM)) for r in firsts], axis=0)
    qr = q * jnp.exp2(jnp.minimum(b - r_full, 0.0))
    k_cols, q_cols = [], []
    for i in range(1, nb):
        n = i * sub
        ki = kk[:n] * jnp.exp2(jnp.minimum(firsts[i] - b[:n], 0.0))
        ki = jnp.concatenate([ki, jnp.zeros((c - n, HGRN_HEAD_DIM), F32)], axis=0)
        k_cols.append(ki.astype(BF16))
        q_cols.append(jnp.where(blk == i, qr, 0.0).astype(BF16))
    return lax.dot_general(jnp.concatenate(k_cols, axis=1), jnp.concatenate(q_cols, axis=1),
                           (((1,), (1,)), ((), ())), preferred_element_type=F32)


HGRN_HEADS_PER_STEP = 4


def _hgrn_body(q_ref, k_ref, lf_ref, i_ref, o_ref, state_ref, *, rows):
    n = pl.program_id(2)

    @pl.when(n == 0)
    def _():
        state_ref[...] = jnp.zeros_like(state_ref)

    def chunk_body(ci, carry):
        sl = pl.ds(pl.multiple_of(ci * HGRN_CHUNK, HGRN_CHUNK), HGRN_CHUNK)
        cols = [pl.ds(hh * HGRN_HEAD_DIM, HGRN_HEAD_DIM) for hh in range(HGRN_HEADS_PER_STEP)]
        results = _hgrn_chunks([q_ref[0, sl, cl] for cl in cols], [k_ref[0, sl, cl] for cl in cols],
                               [lf_ref[0, sl, cl] for cl in cols], [i_ref[0, sl, cl] for cl in cols],
                               [state_ref[hh] for hh in range(HGRN_HEADS_PER_STEP)])
        for hh, (o, new_state) in enumerate(results):
            state_ref[hh] = new_state
            o_ref[0, sl, cols[hh]] = o
        return carry

    lax.fori_loop(0, rows // HGRN_CHUNK, chunk_body, 0)


def _hgrn_core(q, k, lf, ii, *, rows=512):
    bsz, s, _ = q.shape
    width = HGRN_HEADS_PER_STEP * HGRN_HEAD_DIM
    blk = pl.BlockSpec((1, rows, width), lambda b, h, n: (b, n, h))
    return pl.pallas_call(
        functools.partial(_hgrn_body, rows=rows),
        grid=(bsz, HGRN_HEADS // HGRN_HEADS_PER_STEP, s // rows),
        in_specs=[blk, blk, blk, blk], out_specs=blk,
        out_shape=jax.ShapeDtypeStruct((bsz, s, HGRN_HEADS * HGRN_HEAD_DIM), F32),
        scratch_shapes=[pltpu.VMEM((HGRN_HEADS_PER_STEP, HGRN_HEAD_DIM, HGRN_HEAD_DIM), F32)],
        name="hgrn_core", compiler_params=_params(("parallel", "parallel", "arbitrary")),
    )(q, k, lf, ii)


def _hgrn_gate_body(o_ref, g_ref, ng_ref, out_ref):
    o = o_ref[...]
    o = o * lax.rsqrt(jnp.mean(o * o, axis=-1, keepdims=True) + RMS_EPS)
    out_ref[...] = ((o * ng_ref[...]) * g_ref[...]).astype(out_ref.dtype)


def _hgrn_gate(o, g, norm_g, *, tm=256):
    m, d = o.shape
    row = pl.BlockSpec((tm, d), lambda i: (i, 0))
    return pl.pallas_call(
        _hgrn_gate_body, grid=(m // tm,),
        in_specs=[row, row, pl.BlockSpec((1, d), lambda i: (0, 0))], out_specs=row,
        out_shape=jax.ShapeDtypeStruct((m, d), BF16),
        name="hgrn_gate", compiler_params=_params(("parallel",)),
    )(o, g, norm_g.reshape(1, d))


TM = 1024
TN = 1024
TN_NARROW = 512
TM_DEEP = 2048
TK_DEEP = 1024


def _conformer_layer(h32, h16, bsz, j, w_in, b_in, w_dw, b_dw, ln_g, ln_b, w_out, b_out):
    m, d = h32.shape
    b_in = b_in.reshape(1, 2 * d)
    tn_glu = TN_NARROW // 2
    u = _matmul(h16, w_in, j, tm=TM, tn=tn_glu, tk=d, n_out_cols=d,
                out_dtypes=[F32], epilogue=_ep_glu, name="conv_in_glu",
                w_block_offsets=(0, d // tn_glu),
                extras=[(b_in[:, :d], "row"), (b_in[:, d:], "row")])[0]
    u = _dwconv_ln_silu(u.reshape(bsz, m // bsz, d), w_dw, b_dw, ln_g, ln_b).reshape(m, d)
    return _matmul(u, w_out, j, tm=TM, tn=TN_NARROW, tk=d, n_out_cols=d,
                   out_dtypes=[F32], epilogue=_ep_residual, name="conv_out_res",
                   extras=[(h32, "tile"), (b_out.reshape(1, d), "row")])[0]


def _retention_layer(h32, h16, bsz, j, cos, sin, w_q, w_k, w_v, w_g, w_o):
    m, d = h32.shape
    s = m // bsz
    rot = [(cos, "side"), (sin, "side")]
    q = _matmul(h16, w_q, j, tm=TM, tn=TN_NARROW, tk=d, n_out_cols=d, out_dtypes=[F32],
                epilogue=functools.partial(_ep_rotary, scale=1.0), name="ret_q", extras=rot)[0]
    k = _matmul(h16, w_k, j, tm=TM, tn=TN_NARROW, tk=d, n_out_cols=d, out_dtypes=[F32],
                epilogue=functools.partial(_ep_rotary, scale=RET_QK_DIM ** -0.5), name="ret_k",
                extras=rot)[0]
    dv = RET_HEADS * RET_V_DIM
    v = _matmul(h16, w_v, j, tm=TM, tn=TN_NARROW, tk=d, n_out_cols=dv, out_dtypes=[BF16],
                epilogue=_ep_plain, name="ret_v")[0]
    g = _matmul(h16, w_g, j, tm=TM, tn=TN_NARROW, tk=d, n_out_cols=dv, out_dtypes=[F32],
                epilogue=_ep_silu, name="ret_g")[0]
    o = _retention_core(q.reshape(bsz, s, d), k.reshape(bsz, s, d), v.reshape(bsz, s, dv),
                        g.reshape(bsz, s, dv)).reshape(m, dv)
    return _matmul(o, w_o, j, tm=TM_DEEP, tn=TN, tk=TK_DEEP, n_out_cols=d,
                   out_dtypes=[F32], epilogue=_ep_plain, name="ret_out")[0]


def _hgrn_layer(h32, h16, bsz, layer, j, lower_bounds, w_q, w_f, w_i, w_g, norm_g, w_o):
    m, d = h32.shape
    s = m // bsz
    q = _matmul(h16, w_q, j, tm=TM, tn=TN_NARROW, tk=d, n_out_cols=d, out_dtypes=[F32],
                epilogue=_ep_silu, name="hgrn_q")[0]
    log_f, k = _matmul(h16, w_f, j, tm=TM, tn=TN_NARROW, tk=d, n_out_cols=d,
                       out_dtypes=[F32, F32],
                       epilogue=functools.partial(_ep_hgrn_forget, layer=layer), name="hgrn_f",
                       extras=[(lower_bounds, "rows")])
    ii = _matmul(h16, w_i, j, tm=TM, tn=TN_NARROW, tk=d, n_out_cols=d, out_dtypes=[F32],
                 epilogue=_ep_plain, name="hgrn_i")[0]
    g = _matmul(h16, w_g, j, tm=TM, tn=TN_NARROW, tk=d, n_out_cols=d, out_dtypes=[F32],
                epilogue=_ep_silu, name="hgrn_g")[0]
    shp = (bsz, s, d)
    o = _hgrn_core(q.reshape(shp), k.reshape(shp), log_f.reshape(shp), ii.reshape(shp)).reshape(m, d)
    o = _hgrn_gate(o, g, norm_g)
    return _matmul(o, w_o, j, tm=TM, tn=TN_NARROW, tk=d, n_out_cols=d,
                   out_dtypes=[F32], epilogue=_ep_residual, name="hgrn_out_res",
                   extras=[(h32, "tile")])[0]


def _mlp_layer(h32, h16, i, w1, w2):
    m, d = h32.shape
    hidden = w1.shape[2]
    a = _matmul(h16, w1, i, tm=TM, tn=TN_NARROW, tk=d, n_out_cols=hidden, out_dtypes=[BF16],
                epilogue=_ep_relu2, name="mlp_up")[0]
    return _matmul(a, w2, i, tm=TM_DEEP, tn=TN, tk=TK_DEEP, n_out_cols=d, out_dtypes=[F32],
                   epilogue=_ep_plain, name="mlp_down")[0]


def kernel(x, positions, conv_w_in, conv_b_in, conv_w_dw, conv_b_dw, conv_ln_g, conv_ln_b,
           conv_w_out, conv_b_out, ret_w_q, ret_w_k, ret_w_v, ret_w_g, ret_w_o,
           hgrn_lower_bounds, hgrn_w_q, hgrn_w_f, hgrn_w_i, hgrn_w_g, hgrn_norm_g, hgrn_w_o,
           mlp_w1, mlp_w2, ln_mix_g, ln_mix_b, ln_mlp_g, ln_mlp_b):
    bsz, s, d = x.shape
    m = bsz * s
    h32 = x.reshape(m, d)
    h16 = h32.astype(BF16)
    cos, sin = _rope_tables(positions)
    for i in range(DEPTH):
        kind, j = i % N_MIXERS, i // N_MIXERS
        if kind == 0:
            z = _conformer_layer(h32, h16, bsz, j, conv_w_in, conv_b_in[j], conv_w_dw[j],
                                 conv_b_dw[j], conv_ln_g[j], conv_ln_b[j], conv_w_out,
                                 conv_b_out[j])
        elif kind == 1:
            z = _retention_layer(h32, h16, bsz, j, cos, sin, ret_w_q, ret_w_k, ret_w_v,
                                 ret_w_g, ret_w_o)
        else:
            z = _hgrn_layer(h32, h16, bsz, i, j, hgrn_lower_bounds, hgrn_w_q, hgrn_w_f,
                            hgrn_w_i, hgrn_w_g, hgrn_norm_g[j], hgrn_w_o)
        h32, h16 = _layer_norm(z, ln_mix_g[i], ln_mix_b[i], residual=h32 if kind == 1 else None)
        y = _mlp_layer(h32, h16, i, mlp_w1, mlp_w2)
        h32, h16 = _layer_norm(y, ln_mlp_g[i], ln_mlp_b[i], residual=h32)
    return h32.reshape(bsz, s, d)
```

```python
import functools
import math

import jax
import jax.numpy as jnp
from jax import lax
from jax.experimental import pallas as pl
from jax.experimental.pallas import tpu as pltpu

D_MODEL = 4096
DEPTH = 4
N_MIXERS = 3
CONV_WIDTH = 31
RET_HEADS = 16
RET_QK_DIM = D_MODEL // RET_HEADS
RET_V_DIM = 2 * D_MODEL // RET_HEADS
RET_CHUNK = 128
ROPE_BASE = 10000.0
HGRN_HEAD_DIM = 128
HGRN_HEADS = D_MODEL // HGRN_HEAD_DIM
HGRN_CHUNK = 64
LN_EPS = 1e-5
RMS_EPS = 1e-6
ALPHA = (2.0 * DEPTH) ** 0.25

F32 = jnp.float32
BF16 = jnp.bfloat16

VMEM_LIMIT_BYTES = 56 * 1024 * 1024
LANES = 128
SUBLANES = 8
CONV_HALO = 32


def _params(sem):
    return pltpu.CompilerParams(dimension_semantics=sem, vmem_limit_bytes=VMEM_LIMIT_BYTES)


def _sigmoid(x):
    return 1.0 / (1.0 + jnp.exp(-x))


def _mm_body(*refs, n_w, n_extra, n_out, nk, epilogue):
    a_ref = refs[0]
    w_refs = refs[1:1 + n_w]
    extra = refs[1 + n_w:1 + n_w + n_extra]
    outs = refs[1 + n_w + n_extra:1 + n_w + n_extra + n_out]
    accs = refs[1 + n_w + n_extra + n_out:]
    if nk == 1:
        vals = [jnp.dot(a_ref[...], w[...].astype(BF16), preferred_element_type=F32)
                for w in w_refs]
        epilogue(vals, extra, outs)
    else:
        k = pl.program_id(2)

        @pl.when(k == 0)
        def _():
            for acc in accs:
                acc[...] = jnp.zeros_like(acc)

        for acc, w in zip(accs, w_refs):
            acc[...] += jnp.dot(a_ref[...], w[...].astype(BF16), preferred_element_type=F32)

        @pl.when(k == nk - 1)
        def _():
            epilogue([acc[...] for acc in accs], extra, outs)


def _matmul(a, w, layer, *, tm, tn, tk, n_out_cols, out_dtypes, epilogue, name,
            w_block_offsets=(0,), extras=()):
    m, kdim = a.shape
    assert m % tm == 0 and kdim % tk == 0 and n_out_cols % tn == 0
    nk = kdim // tk
    grid = (m // tm, n_out_cols // tn, nk)
    in_specs = [pl.BlockSpec((tm, tk), lambda i, j, k: (i, k))]
    operands = [a]
    for off in w_block_offsets:
        in_specs.append(pl.BlockSpec((None, tk, tn), lambda i, j, k, off=off: (layer, k, j + off)))
        operands.append(w)
    for arr, kind in extras:
        if kind == "row" or kind == "rows":
            in_specs.append(pl.BlockSpec((arr.shape[0], tn), lambda i, j, k: (0, j)))
        elif kind == "tile":
            in_specs.append(pl.BlockSpec((tm, tn), lambda i, j, k: (i, j)))
        elif kind == "side":
            in_specs.append(pl.BlockSpec((tm, arr.shape[1]), lambda i, j, k: (i, 0)))
        else:
            raise ValueError(kind)
        operands.append(arr)
    out_shape = [jax.ShapeDtypeStruct((m, n_out_cols), dt) for dt in out_dtypes]
    out_specs = [pl.BlockSpec((tm, tn), lambda i, j, k: (i, j)) for _ in out_dtypes]
    scratch = []
    if nk > 1:
        scratch = [pltpu.VMEM((tm, tn), F32) for _ in w_block_offsets]
    body = functools.partial(
        _mm_body, n_w=len(w_block_offsets), n_extra=len(extras), n_out=len(out_dtypes),
        nk=nk, epilogue=epilogue)
    res = pl.pallas_call(
        body, grid=grid, in_specs=in_specs, out_specs=out_specs, out_shape=out_shape,
        scratch_shapes=scratch, name=name,
        compiler_params=_params(("parallel", "parallel", "arbitrary")),
    )(*operands)
    return res


def _ep_plain(vals, extra, outs):
    outs[0][...] = vals[0].astype(outs[0].dtype)


def _ep_silu(vals, extra, outs):
    y = vals[0]
    outs[0][...] = (y * _sigmoid(y)).astype(outs[0].dtype)


def _ep_relu2(vals, extra, outs):
    y = jnp.maximum(vals[0], 0.0)
    outs[0][...] = (y * y).astype(outs[0].dtype)


def _ep_glu(vals, extra, outs):
    a = vals[0] + extra[0][...]
    g = vals[1] + extra[1][...]
    outs[0][...] = (a * _sigmoid(g)).astype(outs[0].dtype)


def _ep_residual(vals, extra, outs, *, from_ln, has_bias):
    if from_ln:
        z_ref, stats_ref, g_ref, b_ref = extra[:4]
        h = _ln_apply(z_ref[...], stats_ref[:, 0:1], stats_ref[:, 1:2], g_ref[...], b_ref[...])
        rest = extra[4:]
    else:
        h = extra[0][...]
        rest = extra[1:]
    y = vals[0]
    if has_bias:
        y = y + rest[0][...]
    outs[0][...] = ALPHA * h + y


def _residual_matmul(a, w, layer, hidden, *, bias=None, tm, tn, tk, name):
    from_ln = hidden[0] == "ln"
    if from_ln:
        _, z, stats, g, b = hidden
        d = z.shape[1]
        extras = [(z, "tile"), (stats, "side"), (g.reshape(1, d), "row"), (b.reshape(1, d), "row")]
    else:
        d = hidden[1].shape[1]
        extras = [(hidden[1], "tile")]
    if bias is not None:
        extras.append((bias.reshape(1, d), "row"))
    return _matmul(a, w, layer, tm=tm, tn=tn, tk=tk, n_out_cols=d, out_dtypes=[F32],
                   epilogue=functools.partial(_ep_residual, from_ln=from_ln, has_bias=bias is not None),
                   name=name, extras=extras)[0]


def _ep_rotary(vals, extra, outs, *, scale):
    cos = extra[0][...]
    sin = extra[1][...]
    half = RET_QK_DIM // 2
    y = vals[0]
    tn = y.shape[1]
    for h in range(tn // RET_QK_DIM):
        lo = h * RET_QK_DIM
        u1 = y[:, lo:lo + half] * scale
        u2 = y[:, lo + half:lo + 2 * half] * scale
        outs[0][:, lo:lo + half] = u1 * cos - u2 * sin
        outs[0][:, lo + half:lo + 2 * half] = u2 * cos + u1 * sin


def _ep_hgrn_forget(vals, extra, outs, *, layer):
    p = extra[0][...]
    p = p - jnp.max(p, axis=0, keepdims=True)
    e = jnp.exp(p)
    sm = e / jnp.sum(e, axis=0, keepdims=True)
    lb = jnp.sum(sm[1:layer + 1, :], axis=0, keepdims=True)
    z = vals[0]
    log_sig = jnp.minimum(z, 0.0) - jnp.log1p(jnp.exp(-jnp.abs(z)))
    a = jnp.log(lb)
    c = jnp.log1p(-lb) + log_sig
    log_f = jnp.maximum(a, c) + jnp.log1p(jnp.exp(-jnp.abs(a - c)))
    outs[0][...] = log_f
    outs[1][...] = (1.0 - lb) * _sigmoid(-z)


def _ln_rows(u, g, b):
    mu = jnp.mean(u, axis=-1, keepdims=True)
    d = u - mu
    var = jnp.mean(d * d, axis=-1, keepdims=True)
    return d * lax.rsqrt(var + LN_EPS) * g + b


def _ln_apply(u, mu, rstd, g, b):
    return (u - mu) * rstd * g + b


def _ln_stats_body(z_ref, g_ref, b_ref, o16_ref, stats_ref):
    u = z_ref[...]
    mu = jnp.mean(u, axis=-1, keepdims=True)
    d = u - mu
    rstd = lax.rsqrt(jnp.mean(d * d, axis=-1, keepdims=True) + LN_EPS)
    o16_ref[...] = _ln_apply(u, mu, rstd, g_ref[...], b_ref[...]).astype(BF16)
    lane = lax.broadcasted_iota(jnp.int32, stats_ref.shape, 1)
    stats_ref[...] = jnp.where(lane == 0, mu, rstd)


def _ln_final_body(z_ref, g_ref, b_ref, o32_ref):
    u = z_ref[...]
    mu = jnp.mean(u, axis=-1, keepdims=True)
    d = u - mu
    rstd = lax.rsqrt(jnp.mean(d * d, axis=-1, keepdims=True) + LN_EPS)
    o32_ref[...] = _ln_apply(u, mu, rstd, g_ref[...], b_ref[...])


def _layer_norm(z, g, b, *, final=False, tm=256):
    m, d = z.shape
    row = pl.BlockSpec((tm, d), lambda i: (i, 0))
    vec = pl.BlockSpec((1, d), lambda i: (0, 0))
    args = (z, g.reshape(1, d), b.reshape(1, d))
    if final:
        return pl.pallas_call(
            _ln_final_body, grid=(m // tm,), in_specs=[row, vec, vec], out_specs=row,
            out_shape=jax.ShapeDtypeStruct((m, d), F32),
            name="layer_norm_final", compiler_params=_params(("parallel",)))(*args)
    return pl.pallas_call(
        _ln_stats_body, grid=(m // tm,), in_specs=[row, vec, vec],
        out_specs=[row, pl.BlockSpec((tm, LANES), lambda i: (i, 0))],
        out_shape=[jax.ShapeDtypeStruct((m, d), BF16), jax.ShapeDtypeStruct((m, LANES), F32)],
        name="layer_norm", compiler_params=_params(("parallel",)))(*args)


CONV_ROW_TILE = 64
CONV_LN_ROWS = 16


def _dwconv_body(u_ref, halo_ref, w_ref, bdw_ref, g_ref, b_ref, o_ref, cbuf, *, ts):
    s = pl.program_id(1)
    d = u_ref.shape[2]
    window = CONV_HALO + CONV_ROW_TILE
    first = CONV_HALO - (CONV_WIDTH - 1)

    def col_body(c, carry):
        cols = pl.ds(pl.multiple_of(c * LANES, LANES), LANES)
        w = w_ref[:, cols]
        bias = bdw_ref[:, cols]
        for r in range(ts // CONV_ROW_TILE):
            r0 = r * CONV_ROW_TILE
            if r == 0:
                halo = jnp.where(s > 0, halo_ref[0, :, cols], 0.0)
                win = jnp.concatenate([halo, u_ref[0, 0:CONV_ROW_TILE, cols]], axis=0)
            else:
                win = u_ref[0, r0 - CONV_HALO:r0 + CONV_ROW_TILE, cols]
            acc = jnp.broadcast_to(bias, (CONV_ROW_TILE, LANES))
            for sh in range(SUBLANES):
                shifted = win if sh == 0 else pltpu.roll(win, window - sh, 0)
                for j in range(CONV_WIDTH):
                    if (first + j) % SUBLANES == sh:
                        a = (first + j) - sh
                        acc = acc + w[j:j + 1, :] * shifted[a:a + CONV_ROW_TILE]
            cbuf[r0:r0 + CONV_ROW_TILE, cols] = acc
        return carry

    lax.fori_loop(0, d // LANES, col_body, 0)

    def ln_body(i, carry):
        rows = pl.ds(pl.multiple_of(i * CONV_LN_ROWS, CONV_LN_ROWS), CONV_LN_ROWS)
        y = _ln_rows(cbuf[rows, :], g_ref[...], b_ref[...])
        o_ref[0, rows, :] = (y * _sigmoid(y)).astype(o_ref.dtype)
        return carry

    lax.fori_loop(0, ts // CONV_LN_ROWS, ln_body, 0, unroll=2)


def _dwconv_ln_silu(u, w_dw, b_dw, ln_g, ln_b, *, ts=256):
    bsz, s, d = u.shape
    per = ts // CONV_HALO
    vec = pl.BlockSpec((1, d), lambda b, i: (0, 0))
    return pl.pallas_call(
        functools.partial(_dwconv_body, ts=ts),
        grid=(bsz, s // ts),
        in_specs=[
            pl.BlockSpec((1, ts, d), lambda b, i: (b, i, 0)),
            pl.BlockSpec((1, CONV_HALO, d), lambda b, i: (b, jnp.maximum(i * per - 1, 0), 0)),
            pl.BlockSpec((CONV_WIDTH, d), lambda b, i: (0, 0)),
            vec, vec, vec,
        ],
        out_specs=pl.BlockSpec((1, ts, d), lambda b, i: (b, i, 0)),
        out_shape=jax.ShapeDtypeStruct((bsz, s, d), BF16),
        scratch_shapes=[pltpu.VMEM((ts, d), F32)],
        name="dwconv_ln_silu", compiler_params=_params(("parallel", "arbitrary")),
    )(u, u, w_dw, b_dw.reshape(1, d), ln_g.reshape(1, d), ln_b.reshape(1, d))


def _rope_body(pos_ref, cos_ref, sin_ref):
    half = cos_ref.shape[1]
    lane = lax.broadcasted_iota(jnp.int32, (1, half), 1).astype(F32)
    inv_freq = jnp.exp(lane * (-math.log(ROPE_BASE) / (half - 1)))
    ang = pos_ref[...].astype(F32) * inv_freq
    cos_ref[...] = jnp.cos(ang)
    sin_ref[...] = jnp.sin(ang)


def _rope_tables(positions, *, tm=1024):
    m = positions.size
    half = RET_QK_DIM // 2
    out = pl.BlockSpec((tm, half), lambda i: (i, 0))
    return pl.pallas_call(
        _rope_body, grid=(m // tm,),
        in_specs=[pl.BlockSpec((tm, 1), lambda i: (i, 0))],
        out_specs=[out, out],
        out_shape=[jax.ShapeDtypeStruct((m, half), F32)] * 2,
        name="rope_tables", compiler_params=_params(("parallel",)),
    )(positions.reshape(m, 1))


def _ret_body(q_ref, k_ref, v_ref, g_ref, o_ref, state_ref, *, rows):
    h = pl.program_id(1)
    n = pl.program_id(2)
    c = RET_CHUNK

    @pl.when(n == 0)
    def _():
        state_ref[...] = jnp.zeros_like(state_ref)

    hf = jnp.full((1, 1), h, jnp.int32).astype(F32)
    log_gamma = jnp.log1p(-jnp.exp2(-5.0 - hf))
    t_col = lax.broadcasted_iota(jnp.int32, (c, 1), 0).astype(F32)
    t_row = lax.broadcasted_iota(jnp.int32, (1, c), 1).astype(F32)
    diff = t_col - t_row
    intra = jnp.where(diff >= 0, jnp.exp(log_gamma * jnp.maximum(diff, 0.0)), 0.0)
    q_decay = jnp.exp(log_gamma * (t_col + 1.0))
    k_decay = jnp.exp(log_gamma * (c - 1.0 - t_col))
    chunk_decay = jnp.exp(log_gamma * c)

    for ci in range(rows // c):
        sl = pl.ds(ci * c, c)
        qi = q_ref[0, sl, :]
        ki = k_ref[0, sl, :]
        vi = v_ref[0, sl, :].astype(BF16)
        state = state_ref[...]
        scores = lax.dot_general(qi.astype(BF16), ki.astype(BF16), (((1,), (1,)), ((), ())),
                                 preferred_element_type=F32) * intra
        o = jnp.dot(scores.astype(BF16), vi, preferred_element_type=F32)
        o = o + jnp.dot((qi * q_decay).astype(BF16), state.astype(BF16), preferred_element_type=F32)
        state_ref[...] = chunk_decay * state + lax.dot_general(
            (ki * k_decay).astype(BF16), vi, (((0,), (0,)), ((), ())), preferred_element_type=F32)
        o = o * lax.rsqrt(jnp.mean(o * o, axis=-1, keepdims=True) + RMS_EPS)
        o_ref[0, sl, :] = (o * g_ref[0, sl, :]).astype(o_ref.dtype)


def _retention_core(q, k, v, g, *, rows=1024):
    bsz, s, _ = q.shape
    qk = pl.BlockSpec((1, rows, RET_QK_DIM), lambda b, h, n: (b, n, h))
    vv = pl.BlockSpec((1, rows, RET_V_DIM), lambda b, h, n: (b, n, h))
    return pl.pallas_call(
        functools.partial(_ret_body, rows=rows),
        grid=(bsz, RET_HEADS, s // rows),
        in_specs=[qk, qk, vv, vv], out_specs=vv,
        out_shape=jax.ShapeDtypeStruct((bsz, s, RET_HEADS * RET_V_DIM), BF16),
        scratch_shapes=[pltpu.VMEM((RET_QK_DIM, RET_V_DIM), F32)],
        name="retention_core", compiler_params=_params(("parallel", "parallel", "arbitrary")),
    )(q, k, v, g)


LOG2E = math.log2(math.e)


def _hgrn_chunks(qs, kks, lfs, iis, states_t, b_scr, q_scr):
    c = HGRN_CHUNK
    sub = SUBLANES
    nb = c // sub
    heads = range(len(qs))
    dk = HGRN_HEAD_DIM

    row8 = lax.broadcasted_iota(jnp.int32, (sub, dk), 0)
    scan_masks = [(sh, row8 >= sh) for sh in (1, 2, 4)]
    b2s = []
    for h in heads:
        blocks, total = [], None
        for i in range(nb):
            x = lfs[h][i * sub:(i + 1) * sub] * LOG2E
            for sh, mask in scan_masks:
                x = x + jnp.where(mask, pltpu.roll(x, sh, 0), 0.0)
            if total is not None:
                x = x + total
            total = jnp.broadcast_to(x[sub - 1:sub, :], (sub, dk))
            blocks.append(x)
            b_scr[h, i * sub:(i + 1) * sub, :] = x
        b2s.append(blocks)

    outs = [_hgrn_scores_t(qs[h], kks[h], b2s[h]) for h in heads]
    for h in heads:
        q_scr[h] = qs[h]
    lane_t8 = lax.broadcasted_iota(jnp.int32, (sub, c), 1)
    diag = [[jnp.zeros((sub, c), F32) for _ in range(nb)] for _ in heads]
    for i in range(nb):
        for tt in range(sub):
            t = i * sub + tt
            hit = lane_t8 == t
            for h in heads:
                b_i = b2s[h][i]
                k_i = kks[h][i * sub:(i + 1) * sub]
                w = (k_i * q_scr[h, t:t + 1, :]) * jnp.exp2(jnp.minimum(b_scr[h, t:t + 1, :] - b_i, 0.0))
                diag[h][i] = jnp.where(hit, jnp.sum(w, axis=-1, keepdims=True), diag[h][i])
    lane_t = lax.broadcasted_iota(jnp.int32, (c, c), 1)
    row_s = lax.broadcasted_iota(jnp.int32, (c, c), 0)
    causal = row_s <= lane_t

    results = []
    for h in heads:
        q, kk, ii, state_t = qs[h], kks[h], iis[h], states_t[h]
        b2 = jnp.concatenate(b2s[h], axis=0)
        scores_t = jnp.where(causal, jnp.concatenate(diag[h], axis=0), 0.0) + outs[h]
        i16 = ii.astype(BF16)
        o = lax.dot_general(scores_t.astype(BF16), i16, (((0,), (0,)), ((), ())),
                            preferred_element_type=F32)
        o = o + lax.dot_general((q * jnp.exp2(b2)).astype(BF16), state_t.astype(BF16),
                                (((1,), (1,)), ((), ())), preferred_element_type=F32)
        b_last = b2[c - 1:c, :]
        kd = (kk * jnp.exp2(b_last - b2)).astype(BF16)
        new_state_t = jnp.exp2(b_last) * state_t + lax.dot_general(
            i16, kd, (((0,), (0,)), ((), ())), preferred_element_type=F32)
        results.append((o, new_state_t))
    return results


def _hgrn_scores_t(q, kk, b2_blocks):
    c = HGRN_CHUNK
    sub = SUBLANES
    nb = c // sub
    row = lax.broadcasted_iota(jnp.int32, (c, HGRN_HEAD_DIM), 0)
    b = jnp.concatenate(b2_blocks, axis=0)
    blk = row // sub
    firsts = [blk_b[0:1, :] for blk_b in b2_blocks]
    r_full = jnp.concatenate([jnp.broadcast_to(r, (sub, HGRN_HEAD_DIM)) for r in firsts], axis=0)
    qr = q * jnp.exp2(jnp.minimum(b - r_full, 0.0))
    k_cols, q_cols = [], []
    for i in range(1, nb):
        n = i * sub
        ki = kk[:n] * jnp.exp2(jnp.minimum(firsts[i] - b[:n], 0.0))
        ki = jnp.concatenate([ki, jnp.zeros((c - n, HGRN_HEAD_DIM), F32)], axis=0)
        k_cols.append(ki.astype(BF16))
        q_cols.append(jnp.where(blk == i, qr, 0.0).astype(BF16))
    return lax.dot_general(jnp.concatenate(k_cols, axis=1), jnp.concatenate(q_cols, axis=1),
                           (((1,), (1,)), ((), ())), preferred_element_type=F32)


HGRN_HEADS_PER_STEP = 8


def _hgrn_body(q_ref, k_ref, lf_ref, i_ref, o_ref, state_ref, b_scr, q_scr, *, rows):
    n = pl.program_id(2)

    @pl.when(n == 0)
    def _():
        state_ref[...] = jnp.zeros_like(state_ref)

    def chunk_body(ci, carry):
        r0 = pl.multiple_of(ci * HGRN_CHUNK, HGRN_CHUNK)
        sl = pl.ds(r0, HGRN_CHUNK)
        cols = [pl.ds(hh * HGRN_HEAD_DIM, HGRN_HEAD_DIM) for hh in range(HGRN_HEADS_PER_STEP)]
        results = _hgrn_chunks([q_ref[0, sl, cl] for cl in cols], [k_ref[0, sl, cl] for cl in cols],
                               [lf_ref[0, sl, cl] for cl in cols], [i_ref[0, sl, cl] for cl in cols],
                               [state_ref[hh] for hh in range(HGRN_HEADS_PER_STEP)],
                               b_scr, q_scr)
        for hh, (o, new_state) in enumerate(results):
            state_ref[hh] = new_state
            o_ref[0, sl, cols[hh]] = o
        return carry

    lax.fori_loop(0, rows // HGRN_CHUNK, chunk_body, 0)


def _hgrn_core(q, k, lf, ii, *, rows=512):
    bsz, s, _ = q.shape
    width = HGRN_HEADS_PER_STEP * HGRN_HEAD_DIM
    blk = pl.BlockSpec((1, rows, width), lambda b, h, n: (b, n, h))
    return pl.pallas_call(
        functools.partial(_hgrn_body, rows=rows),
        grid=(bsz, HGRN_HEADS // HGRN_HEADS_PER_STEP, s // rows),
        in_specs=[blk, blk, blk, blk], out_specs=blk,
        out_shape=jax.ShapeDtypeStruct((bsz, s, HGRN_HEADS * HGRN_HEAD_DIM), F32),
        scratch_shapes=[pltpu.VMEM((HGRN_HEADS_PER_STEP, HGRN_HEAD_DIM, HGRN_HEAD_DIM), F32),
                        pltpu.VMEM((HGRN_HEADS_PER_STEP, HGRN_CHUNK, HGRN_HEAD_DIM), F32),
                        pltpu.VMEM((HGRN_HEADS_PER_STEP, HGRN_CHUNK, HGRN_HEAD_DIM), F32)],
        name="hgrn_core", compiler_params=_params(("parallel", "parallel", "arbitrary")),
    )(q, k, lf, ii)


def _hgrn_gate_body(o_ref, g_ref, ng_ref, out_ref):
    o = o_ref[...]
    o = o * lax.rsqrt(jnp.mean(o * o, axis=-1, keepdims=True) + RMS_EPS)
    out_ref[...] = ((o * ng_ref[...]) * g_ref[...]).astype(out_ref.dtype)


def _hgrn_gate(o, g, norm_g, *, tm=256):
    m, d = o.shape
    row = pl.BlockSpec((tm, d), lambda i: (i, 0))
    return pl.pallas_call(
        _hgrn_gate_body, grid=(m // tm,),
        in_specs=[row, row, pl.BlockSpec((1, d), lambda i: (0, 0))], out_specs=row,
        out_shape=jax.ShapeDtypeStruct((m, d), BF16),
        name="hgrn_gate", compiler_params=_params(("parallel",)),
    )(o, g, norm_g.reshape(1, d))


TM = 1024
TN = 1024
TN_NARROW = 512
TK_DEEP = 2048


def _conformer_layer(hidden, h16, bsz, j, w_in, b_in, w_dw, b_dw, ln_g, ln_b, w_out, b_out):
    m, d = h16.shape
    b_in = b_in.reshape(1, 2 * d)
    tn_glu = TN_NARROW // 2
    u = _matmul(h16, w_in, j, tm=TM, tn=tn_glu, tk=d, n_out_cols=d,
                out_dtypes=[F32], epilogue=_ep_glu, name="conv_in_glu",
                w_block_offsets=(0, d // tn_glu),
                extras=[(b_in[:, :d], "row"), (b_in[:, d:], "row")])[0]
    u = _dwconv_ln_silu(u.reshape(bsz, m // bsz, d), w_dw, b_dw, ln_g, ln_b).reshape(m, d)
    return _residual_matmul(u, w_out, j, hidden, bias=b_out, tm=TM, tn=TN_NARROW, tk=d,
                            name="conv_out_res")


def _retention_layer(hidden, h16, bsz, j, cos, sin, w_q, w_k, w_v, w_g, w_o):
    m, d = h16.shape
    s = m // bsz
    rot = [(cos, "side"), (sin, "side")]
    q = _matmul(h16, w_q, j, tm=TM, tn=TN_NARROW, tk=d, n_out_cols=d, out_dtypes=[F32],
                epilogue=functools.partial(_ep_rotary, scale=1.0), name="ret_q", extras=rot)[0]
    k = _matmul(h16, w_k, j, tm=TM, tn=TN_NARROW, tk=d, n_out_cols=d, out_dtypes=[F32],
                epilogue=functools.partial(_ep_rotary, scale=RET_QK_DIM ** -0.5), name="ret_k",
                extras=rot)[0]
    dv = RET_HEADS * RET_V_DIM
    v = _matmul(h16, w_v, j, tm=TM, tn=TN_NARROW, tk=d, n_out_cols=dv, out_dtypes=[BF16],
                epilogue=_ep_plain, name="ret_v")[0]
    g = _matmul(h16, w_g, j, tm=TM, tn=TN_NARROW, tk=d, n_out_cols=dv, out_dtypes=[F32],
                epilogue=_ep_silu, name="ret_g")[0]
    o = _retention_core(q.reshape(bsz, s, d), k.reshape(bsz, s, d), v.reshape(bsz, s, dv),
                        g.reshape(bsz, s, dv)).reshape(m, dv)
    return _residual_matmul(o, w_o, j, hidden, tm=TM, tn=TN, tk=TK_DEEP, name="ret_out_res")


def _hgrn_layer(hidden, h16, bsz, layer, j, lower_bounds, w_q, w_f, w_i, w_g, norm_g, w_o):
    m, d = h16.shape
    s = m // bsz
    q = _matmul(h16, w_q, j, tm=TM, tn=TN_NARROW, tk=d, n_out_cols=d, out_dtypes=[F32],
                epilogue=_ep_silu, name="hgrn_q")[0]
    log_f, k = _matmul(h16, w_f, j, tm=TM, tn=TN_NARROW, tk=d, n_out_cols=d,
                       out_dtypes=[F32, F32],
                       epilogue=functools.partial(_ep_hgrn_forget, layer=layer), name="hgrn_f",
                       extras=[(lower_bounds, "rows")])
    ii = _matmul(h16, w_i, j, tm=TM, tn=TN_NARROW, tk=d, n_out_cols=d, out_dtypes=[F32],
                 epilogue=_ep_plain, name="hgrn_i")[0]
    g = _matmul(h16, w_g, j, tm=TM, tn=TN_NARROW, tk=d, n_out_cols=d, out_dtypes=[F32],
                epilogue=_ep_silu, name="hgrn_g")[0]
    shp = (bsz, s, d)
    o = _hgrn_core(q.reshape(shp), k.reshape(shp), log_f.reshape(shp), ii.reshape(shp)).reshape(m, d)
    o = _hgrn_gate(o, g, norm_g)
    return _residual_matmul(o, w_o, j, hidden, tm=TM, tn=TN_NARROW, tk=d, name="hgrn_out_res")


def _mlp_layer(hidden, h16, i, w1, w2):
    m, d = h16.shape
    a = _matmul(h16, w1, i, tm=TM, tn=TN_NARROW, tk=d, n_out_cols=w1.shape[2], out_dtypes=[BF16],
                epilogue=_ep_relu2, name="mlp_up")[0]
    return _residual_matmul(a, w2, i, hidden, tm=TM, tn=TN, tk=TK_DEEP, name="mlp_down_res")


def kernel(x, positions, conv_w_in, conv_b_in, conv_w_dw, conv_b_dw, conv_ln_g, conv_ln_b,
           conv_w_out, conv_b_out, ret_w_q, ret_w_k, ret_w_v, ret_w_g, ret_w_o,
           hgrn_lower_bounds, hgrn_w_q, hgrn_w_f, hgrn_w_i, hgrn_w_g, hgrn_norm_g, hgrn_w_o,
           mlp_w1, mlp_w2, ln_mix_g, ln_mix_b, ln_mlp_g, ln_mlp_b):
    bsz, s, d = x.shape
    m = bsz * s
    x2 = x.reshape(m, d)
    hidden = ("direct", x2)
    h16 = x2.astype(BF16)
    cos, sin = _rope_tables(positions)
    for i in range(DEPTH):
        kind, j = i % N_MIXERS, i // N_MIXERS
        if kind == 0:
            z = _conformer_layer(hidden, h16, bsz, j, conv_w_in, conv_b_in[j], conv_w_dw[j],
                                 conv_b_dw[j], conv_ln_g[j], conv_ln_b[j], conv_w_out,
                                 conv_b_out[j])
        elif kind == 1:
            z = _retention_layer(hidden, h16, bsz, j, cos, sin, ret_w_q, ret_w_k, ret_w_v,
                                 ret_w_g, ret_w_o)
        else:
            z = _hgrn_layer(hidden, h16, bsz, i, j, hgrn_lower_bounds, hgrn_w_q, hgrn_w_f,
                            hgrn_w_i, hgrn_w_g, hgrn_norm_g[j], hgrn_w_o)
        h16, stats = _layer_norm(z, ln_mix_g[i], ln_mix_b[i])
        hidden = ("ln", z, stats, ln_mix_g[i], ln_mix_b[i])
        z = _mlp_layer(hidden, h16, i, mlp_w1, mlp_w2)
        if i == DEPTH - 1:
            return _layer_norm(z, ln_mlp_g[i], ln_mlp_b[i], final=True).reshape(bsz, s, d)
        h16, stats = _layer_norm(z, ln_mlp_g[i], ln_mlp_b[i])
        hidden = ("ln", z, stats, ln_mlp_g[i], ln_mlp_b[i])
```

```python
import functools
import math

import jax
import jax.numpy as jnp
from jax import lax
from jax.experimental import pallas as pl
from jax.experimental.pallas import tpu as pltpu

D_MODEL = 4096
DEPTH = 4
N_MIXERS = 3
CONV_WIDTH = 31
RET_HEADS = 16
RET_QK_DIM = D_MODEL // RET_HEADS
RET_V_DIM = 2 * D_MODEL // RET_HEADS
RET_CHUNK = 128
ROPE_BASE = 10000.0
HGRN_HEAD_DIM = 128
HGRN_HEADS = D_MODEL // HGRN_HEAD_DIM
HGRN_CHUNK = 64
LN_EPS = 1e-5
RMS_EPS = 1e-6
ALPHA = (2.0 * DEPTH) ** 0.25

F32 = jnp.float32
BF16 = jnp.bfloat16

VMEM_LIMIT_BYTES = 56 * 1024 * 1024
LANES = 128
SUBLANES = 8
CONV_HALO = 32


def _params(sem):
    return pltpu.CompilerParams(dimension_semantics=sem, vmem_limit_bytes=VMEM_LIMIT_BYTES)


def _sigmoid(x):
    return 1.0 / (1.0 + jnp.exp(-x))


def _mm_body(*refs, n_w, n_extra, n_out, nk, epilogue):
    a_ref = refs[0]
    w_refs = refs[1:1 + n_w]
    extra = refs[1 + n_w:1 + n_w + n_extra]
    outs = refs[1 + n_w + n_extra:1 + n_w + n_extra + n_out]
    accs = refs[1 + n_w + n_extra + n_out:]
    if nk == 1:
        vals = [jnp.dot(a_ref[...], w[...].astype(BF16), preferred_element_type=F32)
                for w in w_refs]
        epilogue(vals, extra, outs)
    else:
        k = pl.program_id(2)

        @pl.when(k == 0)
        def _():
            for acc in accs:
                acc[...] = jnp.zeros_like(acc)

        for acc, w in zip(accs, w_refs):
            acc[...] += jnp.dot(a_ref[...], w[...].astype(BF16), preferred_element_type=F32)

        @pl.when(k == nk - 1)
        def _():
            epilogue([acc[...] for acc in accs], extra, outs)


def _matmul(a, w, layer, *, tm, tn, tk, n_out_cols, out_dtypes, epilogue, name,
            w_block_offsets=(0,), extras=()):
    m, kdim = a.shape
    assert m % tm == 0 and kdim % tk == 0 and n_out_cols % tn == 0
    nk = kdim // tk
    grid = (m // tm, n_out_cols // tn, nk)
    in_specs = [pl.BlockSpec((tm, tk), lambda i, j, k: (i, k))]
    operands = [a]
    for off in w_block_offsets:
        in_specs.append(pl.BlockSpec((None, tk, tn), lambda i, j, k, off=off: (layer, k, j + off)))
        operands.append(w)
    for arr, kind in extras:
        if kind == "row" or kind == "rows":
            in_specs.append(pl.BlockSpec((arr.shape[0], tn), lambda i, j, k: (0, j)))
        elif kind == "tile":
            in_specs.append(pl.BlockSpec((tm, tn), lambda i, j, k: (i, j)))
        elif kind == "side":
            in_specs.append(pl.BlockSpec((tm, arr.shape[1]), lambda i, j, k: (i, 0)))
        else:
            raise ValueError(kind)
        operands.append(arr)
    out_shape = [jax.ShapeDtypeStruct((m, n_out_cols), dt) for dt in out_dtypes]
    out_specs = [pl.BlockSpec((tm, tn), lambda i, j, k: (i, j)) for _ in out_dtypes]
    scratch = []
    if nk > 1:
        scratch = [pltpu.VMEM((tm, tn), F32) for _ in w_block_offsets]
    body = functools.partial(
        _mm_body, n_w=len(w_block_offsets), n_extra=len(extras), n_out=len(out_dtypes),
        nk=nk, epilogue=epilogue)
    res = pl.pallas_call(
        body, grid=grid, in_specs=in_specs, out_specs=out_specs, out_shape=out_shape,
        scratch_shapes=scratch, name=name,
        compiler_params=_params(("parallel", "parallel", "arbitrary")),
    )(*operands)
    return res


def _ep_plain(vals, extra, outs):
    outs[0][...] = vals[0].astype(outs[0].dtype)


def _ep_silu(vals, extra, outs):
    y = vals[0]
    outs[0][...] = (y * _sigmoid(y)).astype(outs[0].dtype)


def _ep_relu2(vals, extra, outs):
    y = jnp.maximum(vals[0], 0.0)
    outs[0][...] = (y * y).astype(outs[0].dtype)


def _ep_glu(vals, extra, outs):
    a = vals[0] + extra[0][...]
    g = vals[1] + extra[1][...]
    outs[0][...] = (a * _sigmoid(g)).astype(outs[0].dtype)


def _ep_residual(vals, extra, outs, *, from_ln, has_bias):
    if from_ln:
        z_ref, stats_ref, g_ref, b_ref = extra[:4]
        h = _ln_apply(z_ref[...], stats_ref[:, 0:1], stats_ref[:, 1:2], g_ref[...], b_ref[...])
        rest = extra[4:]
    else:
        h = extra[0][...]
        rest = extra[1:]
    y = vals[0]
    if has_bias:
        y = y + rest[0][...]
    outs[0][...] = ALPHA * h + y


def _residual_matmul(a, w, layer, hidden, *, bias=None, tm, tn, tk, name):
    from_ln = hidden[0] == "ln"
    if from_ln:
        _, z, stats, g, b = hidden
        d = z.shape[1]
        extras = [(z, "tile"), (stats, "side"), (g.reshape(1, d), "row"), (b.reshape(1, d), "row")]
    else:
        d = hidden[1].shape[1]
        extras = [(hidden[1], "tile")]
    if bias is not None:
        extras.append((bias.reshape(1, d), "row"))
    return _matmul(a, w, layer, tm=tm, tn=tn, tk=tk, n_out_cols=d, out_dtypes=[F32],
                   epilogue=functools.partial(_ep_residual, from_ln=from_ln, has_bias=bias is not None),
                   name=name, extras=extras)[0]


def _ep_rotary(vals, extra, outs, *, scale):
    cos = extra[0][...]
    sin = extra[1][...]
    half = RET_QK_DIM // 2
    y = vals[0]
    tn = y.shape[1]
    for h in range(tn // RET_QK_DIM):
        lo = h * RET_QK_DIM
        u1 = y[:, lo:lo + half] * scale
        u2 = y[:, lo + half:lo + 2 * half] * scale
        outs[0][:, lo:lo + half] = u1 * cos - u2 * sin
        outs[0][:, lo + half:lo + 2 * half] = u2 * cos + u1 * sin


def _ep_hgrn_forget(vals, extra, outs, *, layer):
    p = extra[0][...]
    p = p - jnp.max(p, axis=0, keepdims=True)
    e = jnp.exp(p)
    sm = e / jnp.sum(e, axis=0, keepdims=True)
    lb = jnp.sum(sm[1:layer + 1, :], axis=0, keepdims=True)
    z = vals[0]
    e = jnp.exp(-jnp.abs(z))
    one_e = 1.0 + e
    log_sig = jnp.minimum(z, 0.0) - jnp.log(one_e)
    a = jnp.log(lb)
    c = jnp.log1p(-lb) + log_sig
    log_f = jnp.maximum(a, c) + jnp.log(1.0 + jnp.exp(-jnp.abs(a - c)))
    outs[0][...] = log_f
    outs[1][...] = (1.0 - lb) * (jnp.where(z >= 0.0, e, 1.0) / one_e)


def _ln_rows(u, g, b):
    mu = jnp.mean(u, axis=-1, keepdims=True)
    d = u - mu
    var = jnp.mean(d * d, axis=-1, keepdims=True)
    return d * lax.rsqrt(var + LN_EPS) * g + b


def _ln_apply(u, mu, rstd, g, b):
    return (u - mu) * rstd * g + b


def _ln_stats_body(z_ref, g_ref, b_ref, o16_ref, stats_ref):
    u = z_ref[...]
    mu = jnp.mean(u, axis=-1, keepdims=True)
    d = u - mu
    rstd = lax.rsqrt(jnp.mean(d * d, axis=-1, keepdims=True) + LN_EPS)
    o16_ref[...] = _ln_apply(u, mu, rstd, g_ref[...], b_ref[...]).astype(BF16)
    lane = lax.broadcasted_iota(jnp.int32, stats_ref.shape, 1)
    stats_ref[...] = jnp.where(lane == 0, mu, rstd)


def _ln_final_body(z_ref, g_ref, b_ref, o32_ref):
    u = z_ref[...]
    mu = jnp.mean(u, axis=-1, keepdims=True)
    d = u - mu
    rstd = lax.rsqrt(jnp.mean(d * d, axis=-1, keepdims=True) + LN_EPS)
    o32_ref[...] = _ln_apply(u, mu, rstd, g_ref[...], b_ref[...])


def _layer_norm(z, g, b, *, final=False, tm=512):
    m, d = z.shape
    row = pl.BlockSpec((tm, d), lambda i: (i, 0))
    vec = pl.BlockSpec((1, d), lambda i: (0, 0))
    args = (z, g.reshape(1, d), b.reshape(1, d))
    if final:
        return pl.pallas_call(
            _ln_final_body, grid=(m // tm,), in_specs=[row, vec, vec], out_specs=row,
            out_shape=jax.ShapeDtypeStruct((m, d), F32),
            name="layer_norm_final", compiler_params=_params(("parallel",)))(*args)
    return pl.pallas_call(
        _ln_stats_body, grid=(m // tm,), in_specs=[row, vec, vec],
        out_specs=[row, pl.BlockSpec((tm, LANES), lambda i: (i, 0))],
        out_shape=[jax.ShapeDtypeStruct((m, d), BF16), jax.ShapeDtypeStruct((m, LANES), F32)],
        name="layer_norm", compiler_params=_params(("parallel",)))(*args)


CONV_ROW_TILE = 64
CONV_LN_ROWS = 16


def _dwconv_body(u_ref, halo_ref, w_ref, bdw_ref, g_ref, b_ref, o_ref, cbuf, *, ts):
    s = pl.program_id(1)
    d = u_ref.shape[2]
    window = CONV_HALO + CONV_ROW_TILE
    first = CONV_HALO - (CONV_WIDTH - 1)

    def col_body(c, carry):
        cols = pl.ds(pl.multiple_of(c * LANES, LANES), LANES)
        w = w_ref[:, cols]
        bias = bdw_ref[:, cols]
        for r in range(ts // CONV_ROW_TILE):
            r0 = r * CONV_ROW_TILE
            if r == 0:
                halo = jnp.where(s > 0, halo_ref[0, :, cols], 0.0)
                win = jnp.concatenate([halo, u_ref[0, 0:CONV_ROW_TILE, cols]], axis=0)
            else:
                win = u_ref[0, r0 - CONV_HALO:r0 + CONV_ROW_TILE, cols]
            acc = jnp.broadcast_to(bias, (CONV_ROW_TILE, LANES))
            for sh in range(SUBLANES):
                shifted = win if sh == 0 else pltpu.roll(win, window - sh, 0)
                for j in range(CONV_WIDTH):
                    if (first + j) % SUBLANES == sh:
                        a = (first + j) - sh
                        acc = acc + w[j:j + 1, :] * shifted[a:a + CONV_ROW_TILE]
            cbuf[r0:r0 + CONV_ROW_TILE, cols] = acc
        return carry

    lax.fori_loop(0, d // LANES, col_body, 0)

    def ln_body(i, carry):
        rows = pl.ds(pl.multiple_of(i * CONV_LN_ROWS, CONV_LN_ROWS), CONV_LN_ROWS)
        y = _ln_rows(cbuf[rows, :], g_ref[...], b_ref[...])
        o_ref[0, rows, :] = (y * _sigmoid(y)).astype(o_ref.dtype)
        return carry

    lax.fori_loop(0, ts // CONV_LN_ROWS, ln_body, 0, unroll=4)


def _dwconv_ln_silu(u, w_dw, b_dw, ln_g, ln_b, *, ts=256):
    bsz, s, d = u.shape
    per = ts // CONV_HALO
    vec = pl.BlockSpec((1, d), lambda b, i: (0, 0))
    return pl.pallas_call(
        functools.partial(_dwconv_body, ts=ts),
        grid=(bsz, s // ts),
        in_specs=[
            pl.BlockSpec((1, ts, d), lambda b, i: (b, i, 0)),
            pl.BlockSpec((1, CONV_HALO, d), lambda b, i: (b, jnp.maximum(i * per - 1, 0), 0)),
            pl.BlockSpec((CONV_WIDTH, d), lambda b, i: (0, 0)),
            vec, vec, vec,
        ],
        out_specs=pl.BlockSpec((1, ts, d), lambda b, i: (b, i, 0)),
        out_shape=jax.ShapeDtypeStruct((bsz, s, d), BF16),
        scratch_shapes=[pltpu.VMEM((ts, d), F32)],
        name="dwconv_ln_silu", compiler_params=_params(("parallel", "arbitrary")),
    )(u, u, w_dw, b_dw.reshape(1, d), ln_g.reshape(1, d), ln_b.reshape(1, d))


def _rope_body(pos_ref, cos_ref, sin_ref):
    half = cos_ref.shape[1]
    lane = lax.broadcasted_iota(jnp.int32, (1, half), 1).astype(F32)
    inv_freq = jnp.exp(lane * (-math.log(ROPE_BASE) / (half - 1)))
    ang = pos_ref[...].astype(F32) * inv_freq
    cos_ref[...] = jnp.cos(ang)
    sin_ref[...] = jnp.sin(ang)


def _rope_tables(positions, *, tm=1024):
    m = positions.size
    half = RET_QK_DIM // 2
    out = pl.BlockSpec((tm, half), lambda i: (i, 0))
    return pl.pallas_call(
        _rope_body, grid=(m // tm,),
        in_specs=[pl.BlockSpec((tm, 1), lambda i: (i, 0))],
        out_specs=[out, out],
        out_shape=[jax.ShapeDtypeStruct((m, half), F32)] * 2,
        name="rope_tables", compiler_params=_params(("parallel",)),
    )(positions.reshape(m, 1))


def _ret_body(q_ref, k_ref, v_ref, g_ref, o_ref, state_ref, *, rows):
    h = pl.program_id(1)
    n = pl.program_id(2)
    c = RET_CHUNK

    @pl.when(n == 0)
    def _():
        state_ref[...] = jnp.zeros_like(state_ref)

    hf = jnp.full((1, 1), h, jnp.int32).astype(F32)
    log_gamma = jnp.log1p(-jnp.exp2(-5.0 - hf))
    t_col = lax.broadcasted_iota(jnp.int32, (c, 1), 0).astype(F32)
    t_row = lax.broadcasted_iota(jnp.int32, (1, c), 1).astype(F32)
    diff = t_col - t_row
    intra = jnp.where(diff >= 0, jnp.exp(log_gamma * jnp.maximum(diff, 0.0)), 0.0)
    q_decay = jnp.exp(log_gamma * (t_col + 1.0))
    k_decay = jnp.exp(log_gamma * (c - 1.0 - t_col))
    chunk_decay = jnp.exp(log_gamma * c)

    for ci in range(rows // c):
        sl = pl.ds(ci * c, c)
        qi = q_ref[0, sl, :]
        ki = k_ref[0, sl, :]
        vi = v_ref[0, sl, :].astype(BF16)
        state = state_ref[...]
        scores = lax.dot_general(qi.astype(BF16), ki.astype(BF16), (((1,), (1,)), ((), ())),
                                 preferred_element_type=F32) * intra
        o = jnp.dot(scores.astype(BF16), vi, preferred_element_type=F32)
        o = o + jnp.dot((qi * q_decay).astype(BF16), state.astype(BF16), preferred_element_type=F32)
        state_ref[...] = chunk_decay * state + lax.dot_general(
            (ki * k_decay).astype(BF16), vi, (((0,), (0,)), ((), ())), preferred_element_type=F32)
        o = o * lax.rsqrt(jnp.mean(o * o, axis=-1, keepdims=True) + RMS_EPS)
        o_ref[0, sl, :] = (o * g_ref[0, sl, :]).astype(o_ref.dtype)


def _retention_core(q, k, v, g, *, rows=1024):
    bsz, s, _ = q.shape
    qk = pl.BlockSpec((1, rows, RET_QK_DIM), lambda b, h, n: (b, n, h))
    vv = pl.BlockSpec((1, rows, RET_V_DIM), lambda b, h, n: (b, n, h))
    return pl.pallas_call(
        functools.partial(_ret_body, rows=rows),
        grid=(bsz, RET_HEADS, s // rows),
        in_specs=[qk, qk, vv, vv], out_specs=vv,
        out_shape=jax.ShapeDtypeStruct((bsz, s, RET_HEADS * RET_V_DIM), BF16),
        scratch_shapes=[pltpu.VMEM((RET_QK_DIM, RET_V_DIM), F32)],
        name="retention_core", compiler_params=_params(("parallel", "parallel", "arbitrary")),
    )(q, k, v, g)


LOG2E = math.log2(math.e)


def _hgrn_chunks(qs, kks, lfs, iis, states_t, b_scr, q_scr):
    c = HGRN_CHUNK
    sub = SUBLANES
    nb = c // sub
    heads = range(len(qs))
    dk = HGRN_HEAD_DIM

    row8 = lax.broadcasted_iota(jnp.int32, (sub, dk), 0)
    scan_masks = [(sh, row8 >= sh) for sh in (1, 2, 4)]
    b2s = []
    for h in heads:
        blocks, total = [], None
        for i in range(nb):
            x = lfs[h][i * sub:(i + 1) * sub] * LOG2E
            for sh, mask in scan_masks:
                x = x + jnp.where(mask, pltpu.roll(x, sh, 0), 0.0)
            if total is not None:
                x = x + total
            total = jnp.broadcast_to(x[sub - 1:sub, :], (sub, dk))
            blocks.append(x)
            b_scr[h, i * sub:(i + 1) * sub, :] = x
        b2s.append(blocks)

    outs = [_hgrn_scores_t(qs[h], kks[h], b2s[h]) for h in heads]
    for h in heads:
        q_scr[h] = qs[h]
    lane_t8 = lax.broadcasted_iota(jnp.int32, (sub, c), 1)
    diag = [[jnp.zeros((sub, c), F32) for _ in range(nb)] for _ in heads]
    for i in range(nb):
        for tt in range(sub):
            t = i * sub + tt
            hit = lane_t8 == t
            for h in heads:
                b_i = b2s[h][i]
                k_i = kks[h][i * sub:(i + 1) * sub]
                w = (k_i * q_scr[h, t:t + 1, :]) * jnp.exp2(jnp.minimum(b_scr[h, t:t + 1, :] - b_i, 0.0))
                diag[h][i] = jnp.where(hit, jnp.sum(w, axis=-1, keepdims=True), diag[h][i])
    lane_t = lax.broadcasted_iota(jnp.int32, (c, c), 1)
    row_s = lax.broadcasted_iota(jnp.int32, (c, c), 0)
    causal = row_s <= lane_t

    results = []
    for h in heads:
        q, kk, ii, state_t = qs[h], kks[h], iis[h], states_t[h]
        b2 = jnp.concatenate(b2s[h], axis=0)
        scores_t = jnp.where(causal, jnp.concatenate(diag[h], axis=0), 0.0) + outs[h]
        i16 = ii.astype(BF16)
        o = lax.dot_general(scores_t.astype(BF16), i16, (((0,), (0,)), ((), ())),
                            preferred_element_type=F32)
        o = o + lax.dot_general((q * jnp.exp2(b2)).astype(BF16), state_t.astype(BF16),
                                (((1,), (1,)), ((), ())), preferred_element_type=F32)
        b_last = b2[c - 1:c, :]
        kd = (kk * jnp.exp2(b_last - b2)).astype(BF16)
        new_state_t = jnp.exp2(b_last) * state_t + lax.dot_general(
            i16, kd, (((0,), (0,)), ((), ())), preferred_element_type=F32)
        results.append((o, new_state_t))
    return results


def _hgrn_scores_t(q, kk, b2_blocks):
    c = HGRN_CHUNK
    sub = SUBLANES
    nb = c // sub
    row = lax.broadcasted_iota(jnp.int32, (c, HGRN_HEAD_DIM), 0)
    b = jnp.concatenate(b2_blocks, axis=0)
    blk = row // sub
    firsts = [blk_b[0:1, :] for blk_b in b2_blocks]
    r_full = jnp.concatenate([jnp.broadcast_to(r, (sub, HGRN_HEAD_DIM)) for r in firsts], axis=0)
    qr = q * jnp.exp2(jnp.minimum(b - r_full, 0.0))
    k_cols, q_cols = [], []
    for i in range(1, nb):
        n = i * sub
        ki = kk[:n] * jnp.exp2(jnp.minimum(firsts[i] - b[:n], 0.0))
        ki = jnp.concatenate([ki, jnp.zeros((c - n, HGRN_HEAD_DIM), F32)], axis=0)
        k_cols.append(ki.astype(BF16))
        q_cols.append(jnp.where(blk == i, qr, 0.0).astype(BF16))
    return lax.dot_general(jnp.concatenate(k_cols, axis=1), jnp.concatenate(q_cols, axis=1),
                           (((1,), (1,)), ((), ())), preferred_element_type=F32)


HGRN_HEADS_PER_STEP = 16


def _hgrn_body(q_ref, k_ref, lf_ref, i_ref, o_ref, state_ref, b_scr, q_scr, *, rows):
    n = pl.program_id(2)

    @pl.when(n == 0)
    def _():
        state_ref[...] = jnp.zeros_like(state_ref)

    def chunk_body(ci, carry):
        r0 = pl.multiple_of(ci * HGRN_CHUNK, HGRN_CHUNK)
        sl = pl.ds(r0, HGRN_CHUNK)
        cols = [pl.ds(hh * HGRN_HEAD_DIM, HGRN_HEAD_DIM) for hh in range(HGRN_HEADS_PER_STEP)]
        results = _hgrn_chunks([q_ref[0, sl, cl] for cl in cols], [k_ref[0, sl, cl] for cl in cols],
                               [lf_ref[0, sl, cl] for cl in cols], [i_ref[0, sl, cl] for cl in cols],
                               [state_ref[hh] for hh in range(HGRN_HEADS_PER_STEP)],
                               b_scr, q_scr)
        for hh, (o, new_state) in enumerate(results):
            state_ref[hh] = new_state
            o_ref[0, sl, cols[hh]] = o
        return carry

    lax.fori_loop(0, rows // HGRN_CHUNK, chunk_body, 0)


def _hgrn_core(q, k, lf, ii, *, rows=512):
    bsz, s, _ = q.shape
    width = HGRN_HEADS_PER_STEP * HGRN_HEAD_DIM
    blk = pl.BlockSpec((1, rows, width), lambda b, h, n: (b, n, h))
    return pl.pallas_call(
        functools.partial(_hgrn_body, rows=rows),
        grid=(bsz, HGRN_HEADS // HGRN_HEADS_PER_STEP, s // rows),
        in_specs=[blk, blk, blk, blk], out_specs=blk,
        out_shape=jax.ShapeDtypeStruct((bsz, s, HGRN_HEADS * HGRN_HEAD_DIM), F32),
        scratch_shapes=[pltpu.VMEM((HGRN_HEADS_PER_STEP, HGRN_HEAD_DIM, HGRN_HEAD_DIM), F32),
                        pltpu.VMEM((HGRN_HEADS_PER_STEP, HGRN_CHUNK, HGRN_HEAD_DIM), F32),
                        pltpu.VMEM((HGRN_HEADS_PER_STEP, HGRN_CHUNK, HGRN_HEAD_DIM), F32)],
        name="hgrn_core", compiler_params=_params(("parallel", "parallel", "arbitrary")),
    )(q, k, lf, ii)


def _hgrn_gate_body(o_ref, g_ref, ng_ref, out_ref):
    o = o_ref[...]
    o = o * lax.rsqrt(jnp.mean(o * o, axis=-1, keepdims=True) + RMS_EPS)
    out_ref[...] = ((o * ng_ref[...]) * g_ref[...]).astype(out_ref.dtype)


def _hgrn_gate(o, g, norm_g, *, tm=512):
    m, d = o.shape
    row = pl.BlockSpec((tm, d), lambda i: (i, 0))
    return pl.pallas_call(
        _hgrn_gate_body, grid=(m // tm,),
        in_specs=[row, row, pl.BlockSpec((1, d), lambda i: (0, 0))], out_specs=row,
        out_shape=jax.ShapeDtypeStruct((m, d), BF16),
        name="hgrn_gate", compiler_params=_params(("parallel",)),
    )(o, g, norm_g.reshape(1, d))


TM = 1024
TN = 1024
TN_NARROW = 512
TK_DEEP = 2048


def _conformer_layer(hidden, h16, bsz, j, w_in, b_in, w_dw, b_dw, ln_g, ln_b, w_out, b_out):
    m, d = h16.shape
    b_in = b_in.reshape(1, 2 * d)
    tn_glu = TN_NARROW // 2
    u = _matmul(h16, w_in, j, tm=TM, tn=tn_glu, tk=d, n_out_cols=d,
                out_dtypes=[F32], epilogue=_ep_glu, name="conv_in_glu",
                w_block_offsets=(0, d // tn_glu),
                extras=[(b_in[:, :d], "row"), (b_in[:, d:], "row")])[0]
    u = _dwconv_ln_silu(u.reshape(bsz, m // bsz, d), w_dw, b_dw, ln_g, ln_b).reshape(m, d)
    return _residual_matmul(u, w_out, j, hidden, bias=b_out, tm=TM, tn=TN_NARROW, tk=d,
                            name="conv_out_res")


def _retention_layer(hidden, h16, bsz, j, cos, sin, w_q, w_k, w_v, w_g, w_o):
    m, d = h16.shape
    s = m // bsz
    rot = [(cos, "side"), (sin, "side")]
    q = _matmul(h16, w_q, j, tm=TM, tn=TN_NARROW, tk=d, n_out_cols=d, out_dtypes=[F32],
                epilogue=functools.partial(_ep_rotary, scale=1.0), name="ret_q", extras=rot)[0]
    k = _matmul(h16, w_k, j, tm=TM, tn=TN_NARROW, tk=d, n_out_cols=d, out_dtypes=[F32],
                epilogue=functools.partial(_ep_rotary, scale=RET_QK_DIM ** -0.5), name="ret_k",
                extras=rot)[0]
    dv = RET_HEADS * RET_V_DIM
    v = _matmul(h16, w_v, j, tm=TM, tn=TN_NARROW, tk=d, n_out_cols=dv, out_dtypes=[BF16],
                epilogue=_ep_plain, name="ret_v")[0]
    g = _matmul(h16, w_g, j, tm=TM, tn=TN_NARROW, tk=d, n_out_cols=dv, out_dtypes=[F32],
                epilogue=_ep_silu, name="ret_g")[0]
    o = _retention_core(q.reshape(bsz, s, d), k.reshape(bsz, s, d), v.reshape(bsz, s, dv),
                        g.reshape(bsz, s, dv)).reshape(m, dv)
    return _residual_matmul(o, w_o, j, hidden, tm=TM, tn=TN, tk=TK_DEEP, name="ret_out_res")


def _hgrn_layer(hidden, h16, bsz, layer, j, lower_bounds, w_q, w_f, w_i, w_g, norm_g, w_o):
    m, d = h16.shape
    s = m // bsz
    q = _matmul(h16, w_q, j, tm=TM, tn=TN_NARROW, tk=d, n_out_cols=d, out_dtypes=[F32],
                epilogue=_ep_silu, name="hgrn_q")[0]
    log_f, k = _matmul(h16, w_f, j, tm=TM, tn=TN_NARROW, tk=d, n_out_cols=d,
                       out_dtypes=[F32, F32],
                       epilogue=functools.partial(_ep_hgrn_forget, layer=layer), name="hgrn_f",
                       extras=[(lower_bounds, "rows")])
    ii = _matmul(h16, w_i, j, tm=TM, tn=TN_NARROW, tk=d, n_out_cols=d, out_dtypes=[F32],
                 epilogue=_ep_plain, name="hgrn_i")[0]
    g = _matmul(h16, w_g, j, tm=TM, tn=TN_NARROW, tk=d, n_out_cols=d, out_dtypes=[F32],
                epilogue=_ep_silu, name="hgrn_g")[0]
    shp = (bsz, s, d)
    o = _hgrn_core(q.reshape(shp), k.reshape(shp), log_f.reshape(shp), ii.reshape(shp)).reshape(m, d)
    o = _hgrn_gate(o, g, norm_g)
    return _residual_matmul(o, w_o, j, hidden, tm=TM, tn=TN_NARROW, tk=d, name="hgrn_out_res")


def _mlp_layer(hidden, h16, i, w1, w2):
    m, d = h16.shape
    a = _matmul(h16, w1, i, tm=TM, tn=TN_NARROW, tk=d, n_out_cols=w1.shape[2], out_dtypes=[BF16],
                epilogue=_ep_relu2, name="mlp_up")[0]
    return _residual_matmul(a, w2, i, hidden, tm=TM, tn=TN, tk=TK_DEEP, name="mlp_down_res")


def kernel(x, positions, conv_w_in, conv_b_in, conv_w_dw, conv_b_dw, conv_ln_g, conv_ln_b,
           conv_w_out, conv_b_out, ret_w_q, ret_w_k, ret_w_v, ret_w_g, ret_w_o,
           hgrn_lower_bounds, hgrn_w_q, hgrn_w_f, hgrn_w_i, hgrn_w_g, hgrn_norm_g, hgrn_w_o,
           mlp_w1, mlp_w2, ln_mix_g, ln_mix_b, ln_mlp_g, ln_mlp_b):
    bsz, s, d = x.shape
    m = bsz * s
    x2 = x.reshape(m, d)
    hidden = ("direct", x2)
    h16 = x2.astype(BF16)
    cos, sin = _rope_tables(positions)
    for i in range(DEPTH):
        kind, j = i % N_MIXERS, i // N_MIXERS
        if kind == 0:
            z = _conformer_layer(hidden, h16, bsz, j, conv_w_in, conv_b_in[j], conv_w_dw[j],
                                 conv_b_dw[j], conv_ln_g[j], conv_ln_b[j], conv_w_out,
                                 conv_b_out[j])
        elif kind == 1:
            z = _retention_layer(hidden, h16, bsz, j, cos, sin, ret_w_q, ret_w_k, ret_w_v,
                                 ret_w_g, ret_w_o)
        else:
            z = _hgrn_layer(hidden, h16, bsz, i, j, hgrn_lower_bounds, hgrn_w_q, hgrn_w_f,
                            hgrn_w_i, hgrn_w_g, hgrn_norm_g[j], hgrn_w_o)
        h16, stats = _layer_norm(z, ln_mix_g[i], ln_mix_b[i])
        hidden = ("ln", z, stats, ln_mix_g[i], ln_mix_b[i])
        z = _mlp_layer(hidden, h16, i, mlp_w1, mlp_w2)
        if i == DEPTH - 1:
            return _layer_norm(z, ln_mlp_g[i], ln_mlp_b[i], final=True).reshape(bsz, s, d)
        h16, stats = _layer_norm(z, ln_mlp_g[i], ln_mlp_b[i])
        hidden = ("ln", z, stats, ln_mlp_g[i], ln_mlp_b[i])
```

```python
import functools
import math

import jax
import jax.numpy as jnp
from jax import lax
from jax.experimental import pallas as pl
from jax.experimental.pallas import tpu as pltpu

D_MODEL = 4096
DEPTH = 4
N_MIXERS = 3
CONV_WIDTH = 31
RET_HEADS = 16
RET_QK_DIM = D_MODEL // RET_HEADS
RET_V_DIM = 2 * D_MODEL // RET_HEADS
RET_CHUNK = 128
ROPE_BASE = 10000.0
HGRN_HEAD_DIM = 128
HGRN_HEADS = D_MODEL // HGRN_HEAD_DIM
HGRN_CHUNK = 64
LN_EPS = 1e-5
RMS_EPS = 1e-6
ALPHA = (2.0 * DEPTH) ** 0.25

F32 = jnp.float32
BF16 = jnp.bfloat16

VMEM_LIMIT_BYTES = 56 * 1024 * 1024
LANES = 128
SUBLANES = 8
CONV_HALO = 32


def _params(sem):
    return pltpu.CompilerParams(dimension_semantics=sem, vmem_limit_bytes=VMEM_LIMIT_BYTES)


def _sigmoid(x):
    return 1.0 / (1.0 + jnp.exp(-x))


def _mm_body(*refs, n_w, n_extra, n_out, nk, epilogue):
    a_ref = refs[0]
    w_refs = refs[1:1 + n_w]
    extra = refs[1 + n_w:1 + n_w + n_extra]
    outs = refs[1 + n_w + n_extra:1 + n_w + n_extra + n_out]
    accs = refs[1 + n_w + n_extra + n_out:]
    if nk == 1:
        vals = [jnp.dot(a_ref[...], w[...].astype(BF16), preferred_element_type=F32)
                for w in w_refs]
        epilogue(vals, extra, outs)
    else:
        k = pl.program_id(2)

        @pl.when(k == 0)
        def _():
            for acc in accs:
                acc[...] = jnp.zeros_like(acc)

        for acc, w in zip(accs, w_refs):
            acc[...] += jnp.dot(a_ref[...], w[...].astype(BF16), preferred_element_type=F32)

        @pl.when(k == nk - 1)
        def _():
            epilogue([acc[...] for acc in accs], extra, outs)


def _matmul(a, w, layer, *, tm, tn, tk, n_out_cols, out_dtypes, epilogue, name,
            w_block_offsets=(0,), extras=()):
    m, kdim = a.shape
    assert m % tm == 0 and kdim % tk == 0 and n_out_cols % tn == 0
    nk = kdim // tk
    grid = (m // tm, n_out_cols // tn, nk)
    in_specs = [pl.BlockSpec((tm, tk), lambda i, j, k: (i, k))]
    operands = [a]
    for off in w_block_offsets:
        in_specs.append(pl.BlockSpec((None, tk, tn), lambda i, j, k, off=off: (layer, k, j + off)))
        operands.append(w)
    for arr, kind in extras:
        if kind == "row" or kind == "rows":
            in_specs.append(pl.BlockSpec((arr.shape[0], tn), lambda i, j, k: (0, j)))
        elif kind == "tile":
            in_specs.append(pl.BlockSpec((tm, tn), lambda i, j, k: (i, j)))
        elif kind == "side":
            in_specs.append(pl.BlockSpec((tm, arr.shape[1]), lambda i, j, k: (i, 0)))
        else:
            raise ValueError(kind)
        operands.append(arr)
    out_shape = [jax.ShapeDtypeStruct((m, n_out_cols), dt) for dt in out_dtypes]
    out_specs = [pl.BlockSpec((tm, tn), lambda i, j, k: (i, j)) for _ in out_dtypes]
    scratch = []
    if nk > 1:
        scratch = [pltpu.VMEM((tm, tn), F32) for _ in w_block_offsets]
    body = functools.partial(
        _mm_body, n_w=len(w_block_offsets), n_extra=len(extras), n_out=len(out_dtypes),
        nk=nk, epilogue=epilogue)
    res = pl.pallas_call(
        body, grid=grid, in_specs=in_specs, out_specs=out_specs, out_shape=out_shape,
        scratch_shapes=scratch, name=name,
        compiler_params=_params(("parallel", "parallel", "arbitrary")),
    )(*operands)
    return res


def _ep_plain(vals, extra, outs):
    outs[0][...] = vals[0].astype(outs[0].dtype)


def _ep_silu(vals, extra, outs):
    y = vals[0]
    outs[0][...] = (y * _sigmoid(y)).astype(outs[0].dtype)


def _ep_relu2(vals, extra, outs):
    y = jnp.maximum(vals[0], 0.0)
    outs[0][...] = (y * y).astype(outs[0].dtype)


def _ep_glu(vals, extra, outs):
    a = vals[0] + extra[0][...]
    g = vals[1] + extra[1][...]
    outs[0][...] = (a * _sigmoid(g)).astype(outs[0].dtype)


def _ep_residual(vals, extra, outs, *, from_ln, has_bias):
    if from_ln:
        z_ref, stats_ref, g_ref, b_ref = extra[:4]
        h = _ln_apply(z_ref[...], stats_ref[:, 0:1], stats_ref[:, 1:2], g_ref[...], b_ref[...])
        rest = extra[4:]
    else:
        h = extra[0][...]
        rest = extra[1:]
    y = vals[0]
    if has_bias:
        y = y + rest[0][...]
    outs[0][...] = ALPHA * h + y


def _residual_matmul(a, w, layer, hidden, *, bias=None, tm, tn, tk, name):
    from_ln = hidden[0] == "ln"
    if from_ln:
        _, z, stats, g, b = hidden
        d = z.shape[1]
        extras = [(z, "tile"), (stats, "side"), (g.reshape(1, d), "row"), (b.reshape(1, d), "row")]
    else:
        d = hidden[1].shape[1]
        extras = [(hidden[1], "tile")]
    if bias is not None:
        extras.append((bias.reshape(1, d), "row"))
    return _matmul(a, w, layer, tm=tm, tn=tn, tk=tk, n_out_cols=d, out_dtypes=[F32],
                   epilogue=functools.partial(_ep_residual, from_ln=from_ln, has_bias=bias is not None),
                   name=name, extras=extras)[0]


def _ep_rotary(vals, extra, outs, *, scale):
    cos = extra[0][...]
    sin = extra[1][...]
    half = RET_QK_DIM // 2
    y = vals[0]
    tn = y.shape[1]
    for h in range(tn // RET_QK_DIM):
        lo = h * RET_QK_DIM
        u1 = y[:, lo:lo + half] * scale
        u2 = y[:, lo + half:lo + 2 * half] * scale
        outs[0][:, lo:lo + half] = u1 * cos - u2 * sin
        outs[0][:, lo + half:lo + 2 * half] = u2 * cos + u1 * sin


def _ep_hgrn_forget(vals, extra, outs, *, layer):
    p = extra[0][...]
    p = p - jnp.max(p, axis=0, keepdims=True)
    e = jnp.exp(p)
    sm = e / jnp.sum(e, axis=0, keepdims=True)
    lb = jnp.sum(sm[1:layer + 1, :], axis=0, keepdims=True)
    z = vals[0]
    e = jnp.exp(-jnp.abs(z))
    one_e = 1.0 + e
    log_sig = jnp.minimum(z, 0.0) - jnp.log(one_e)
    a = jnp.log(lb)
    c = jnp.log1p(-lb) + log_sig
    log_f = jnp.maximum(a, c) + jnp.log(1.0 + jnp.exp(-jnp.abs(a - c)))
    outs[0][...] = log_f
    outs[1][...] = (1.0 - lb) * (jnp.where(z >= 0.0, e, 1.0) / one_e)


def _ln_rows(u, g, b):
    mu = jnp.mean(u, axis=-1, keepdims=True)
    d = u - mu
    var = jnp.mean(d * d, axis=-1, keepdims=True)
    return d * lax.rsqrt(var + LN_EPS) * g + b


def _ln_apply(u, mu, rstd, g, b):
    return (u - mu) * rstd * g + b


def _ln_stats_body(z_ref, g_ref, b_ref, o16_ref, stats_ref):
    u = z_ref[...]
    mu = jnp.mean(u, axis=-1, keepdims=True)
    d = u - mu
    rstd = lax.rsqrt(jnp.mean(d * d, axis=-1, keepdims=True) + LN_EPS)
    o16_ref[...] = _ln_apply(u, mu, rstd, g_ref[...], b_ref[...]).astype(BF16)
    lane = lax.broadcasted_iota(jnp.int32, stats_ref.shape, 1)
    stats_ref[...] = jnp.where(lane == 0, mu, rstd)


def _ln_final_body(z_ref, g_ref, b_ref, o32_ref):
    u = z_ref[...]
    mu = jnp.mean(u, axis=-1, keepdims=True)
    d = u - mu
    rstd = lax.rsqrt(jnp.mean(d * d, axis=-1, keepdims=True) + LN_EPS)
    o32_ref[...] = _ln_apply(u, mu, rstd, g_ref[...], b_ref[...])


def _layer_norm(z, g, b, *, final=False, tm=512):
    m, d = z.shape
    row = pl.BlockSpec((tm, d), lambda i: (i, 0))
    vec = pl.BlockSpec((1, d), lambda i: (0, 0))
    args = (z, g.reshape(1, d), b.reshape(1, d))
    if final:
        return pl.pallas_call(
            _ln_final_body, grid=(m // tm,), in_specs=[row, vec, vec], out_specs=row,
            out_shape=jax.ShapeDtypeStruct((m, d), F32),
            name="layer_norm_final", compiler_params=_params(("parallel",)))(*args)
    return pl.pallas_call(
        _ln_stats_body, grid=(m // tm,), in_specs=[row, vec, vec],
        out_specs=[row, pl.BlockSpec((tm, LANES), lambda i: (i, 0))],
        out_shape=[jax.ShapeDtypeStruct((m, d), BF16), jax.ShapeDtypeStruct((m, LANES), F32)],
        name="layer_norm", compiler_params=_params(("parallel",)))(*args)


CONV_ROW_TILE = 64
CONV_LN_ROWS = 16


def _dwconv_body(u_ref, halo_ref, w_ref, bdw_ref, g_ref, b_ref, o_ref, cbuf, *, ts):
    s = pl.program_id(1)
    d = u_ref.shape[2]
    window = CONV_HALO + CONV_ROW_TILE
    first = CONV_HALO - (CONV_WIDTH - 1)

    def col_body(c, carry):
        cols = pl.ds(pl.multiple_of(c * LANES, LANES), LANES)
        w = w_ref[:, cols]
        bias = bdw_ref[:, cols]
        for r in range(ts // CONV_ROW_TILE):
            r0 = r * CONV_ROW_TILE
            if r == 0:
                halo = jnp.where(s > 0, halo_ref[0, :, cols], 0.0)
                win = jnp.concatenate([halo, u_ref[0, 0:CONV_ROW_TILE, cols]], axis=0)
            else:
                win = u_ref[0, r0 - CONV_HALO:r0 + CONV_ROW_TILE, cols]
            acc = jnp.broadcast_to(bias, (CONV_ROW_TILE, LANES))
            for sh in range(SUBLANES):
                shifted = win if sh == 0 else pltpu.roll(win, window - sh, 0)
                for j in range(CONV_WIDTH):
                    if (first + j) % SUBLANES == sh:
                        a = (first + j) - sh
                        acc = acc + w[j:j + 1, :] * shifted[a:a + CONV_ROW_TILE]
            cbuf[r0:r0 + CONV_ROW_TILE, cols] = acc
        return carry

    lax.fori_loop(0, d // LANES, col_body, 0)

    def ln_body(i, carry):
        rows = pl.ds(pl.multiple_of(i * CONV_LN_ROWS, CONV_LN_ROWS), CONV_LN_ROWS)
        y = _ln_rows(cbuf[rows, :], g_ref[...], b_ref[...])
        o_ref[0, rows, :] = (y * _sigmoid(y)).astype(o_ref.dtype)
        return carry

    lax.fori_loop(0, ts // CONV_LN_ROWS, ln_body, 0, unroll=4)


def _dwconv_ln_silu(u, w_dw, b_dw, ln_g, ln_b, *, ts=256):
    bsz, s, d = u.shape
    per = ts // CONV_HALO
    vec = pl.BlockSpec((1, d), lambda b, i: (0, 0))
    return pl.pallas_call(
        functools.partial(_dwconv_body, ts=ts),
        grid=(bsz, s // ts),
        in_specs=[
            pl.BlockSpec((1, ts, d), lambda b, i: (b, i, 0)),
            pl.BlockSpec((1, CONV_HALO, d), lambda b, i: (b, jnp.maximum(i * per - 1, 0), 0)),
            pl.BlockSpec((CONV_WIDTH, d), lambda b, i: (0, 0)),
            vec, vec, vec,
        ],
        out_specs=pl.BlockSpec((1, ts, d), lambda b, i: (b, i, 0)),
        out_shape=jax.ShapeDtypeStruct((bsz, s, d), BF16),
        scratch_shapes=[pltpu.VMEM((ts, d), F32)],
        name="dwconv_ln_silu", compiler_params=_params(("parallel", "arbitrary")),
    )(u, u, w_dw, b_dw.reshape(1, d), ln_g.reshape(1, d), ln_b.reshape(1, d))


def _rope_body(pos_ref, cos_ref, sin_ref):
    half = cos_ref.shape[1]
    lane = lax.broadcasted_iota(jnp.int32, (1, half), 1).astype(F32)
    inv_freq = jnp.exp(lane * (-math.log(ROPE_BASE) / (half - 1)))
    ang = pos_ref[...].astype(F32) * inv_freq
    cos_ref[...] = jnp.cos(ang)
    sin_ref[...] = jnp.sin(ang)


def _rope_tables(positions, *, tm=1024):
    m = positions.size
    half = RET_QK_DIM // 2
    out = pl.BlockSpec((tm, half), lambda i: (i, 0))
    return pl.pallas_call(
        _rope_body, grid=(m // tm,),
        in_specs=[pl.BlockSpec((tm, 1), lambda i: (i, 0))],
        out_specs=[out, out],
        out_shape=[jax.ShapeDtypeStruct((m, half), F32)] * 2,
        name="rope_tables", compiler_params=_params(("parallel",)),
    )(positions.reshape(m, 1))


def _ret_body(q_ref, k_ref, v_ref, g_ref, o_ref, state_ref, *, rows):
    h = pl.program_id(1)
    n = pl.program_id(2)
    c = RET_CHUNK

    @pl.when(n == 0)
    def _():
        state_ref[...] = jnp.zeros_like(state_ref)

    t_col = lax.broadcasted_iota(jnp.int32, (c, 1), 0).astype(F32)
    t_row = lax.broadcasted_iota(jnp.int32, (1, c), 1).astype(F32)
    diff = t_col - t_row
    decays = []
    for hh in range(RET_HEADS_PER_STEP):
        hf = jnp.full((1, 1), h * RET_HEADS_PER_STEP + hh, jnp.int32).astype(F32)
        log_gamma = jnp.log1p(-jnp.exp2(-5.0 - hf))
        decays.append((
            jnp.where(diff >= 0, jnp.exp(log_gamma * jnp.maximum(diff, 0.0)), 0.0),
            jnp.exp(log_gamma * (t_col + 1.0)),
            jnp.exp(log_gamma * (c - 1.0 - t_col)),
            jnp.exp(log_gamma * c)))

    for ci in range(rows // c):
        sl = pl.ds(ci * c, c)
        for hh, (intra, q_decay, k_decay, chunk_decay) in enumerate(decays):
            qk_cols = pl.ds(hh * RET_QK_DIM, RET_QK_DIM)
            v_cols = pl.ds(hh * RET_V_DIM, RET_V_DIM)
            qi = q_ref[0, sl, qk_cols]
            ki = k_ref[0, sl, qk_cols]
            vi = v_ref[0, sl, v_cols].astype(BF16)
            state = state_ref[hh]
            scores = lax.dot_general(qi.astype(BF16), ki.astype(BF16), (((1,), (1,)), ((), ())),
                                     preferred_element_type=F32) * intra
            o = jnp.dot(scores.astype(BF16), vi, preferred_element_type=F32)
            o = o + jnp.dot((qi * q_decay).astype(BF16), state.astype(BF16),
                            preferred_element_type=F32)
            state_ref[hh] = chunk_decay * state + lax.dot_general(
                (ki * k_decay).astype(BF16), vi, (((0,), (0,)), ((), ())),
                preferred_element_type=F32)
            o = o * lax.rsqrt(jnp.mean(o * o, axis=-1, keepdims=True) + RMS_EPS)
            o_ref[0, sl, v_cols] = (o * g_ref[0, sl, v_cols]).astype(o_ref.dtype)


RET_HEADS_PER_STEP = 2


def _retention_core(q, k, v, g, *, rows=1024):
    bsz, s, _ = q.shape
    qk = pl.BlockSpec((1, rows, RET_HEADS_PER_STEP * RET_QK_DIM), lambda b, h, n: (b, n, h))
    vv = pl.BlockSpec((1, rows, RET_HEADS_PER_STEP * RET_V_DIM), lambda b, h, n: (b, n, h))
    return pl.pallas_call(
        functools.partial(_ret_body, rows=rows),
        grid=(bsz, RET_HEADS // RET_HEADS_PER_STEP, s // rows),
        in_specs=[qk, qk, vv, vv], out_specs=vv,
        out_shape=jax.ShapeDtypeStruct((bsz, s, RET_HEADS * RET_V_DIM), BF16),
        scratch_shapes=[pltpu.VMEM((RET_HEADS_PER_STEP, RET_QK_DIM, RET_V_DIM), F32)],
        name="retention_core", compiler_params=_params(("parallel", "parallel", "arbitrary")),
    )(q, k, v, g)


LOG2E = math.log2(math.e)


def _hgrn_chunks(qs, kks, lfs, iis, states_t, b_scr, q_scr):
    c = HGRN_CHUNK
    sub = SUBLANES
    nb = c // sub
    heads = range(len(qs))
    dk = HGRN_HEAD_DIM

    row8 = lax.broadcasted_iota(jnp.int32, (sub, dk), 0)
    scan_masks = [(sh, row8 >= sh) for sh in (1, 2, 4)]
    b2s = []
    for h in heads:
        blocks, total = [], None
        for i in range(nb):
            x = lfs[h][i * sub:(i + 1) * sub] * LOG2E
            for sh, mask in scan_masks:
                x = x + jnp.where(mask, pltpu.roll(x, sh, 0), 0.0)
            if total is not None:
                x = x + total
            total = jnp.broadcast_to(x[sub - 1:sub, :], (sub, dk))
            blocks.append(x)
            b_scr[h, i * sub:(i + 1) * sub, :] = x
        b2s.append(blocks)

    outs = [_hgrn_scores_t(qs[h], kks[h], b2s[h]) for h in heads]
    for h in heads:
        q_scr[h] = qs[h]
    lane_t8 = lax.broadcasted_iota(jnp.int32, (sub, c), 1)
    diag = [[jnp.zeros((sub, c), F32) for _ in range(nb)] for _ in heads]
    for i in range(nb):
        for tt in range(sub):
            t = i * sub + tt
            hit = lane_t8 == t
            for h in heads:
                b_i = b2s[h][i]
                k_i = kks[h][i * sub:(i + 1) * sub]
                w = (k_i * q_scr[h, t:t + 1, :]) * jnp.exp2(jnp.minimum(b_scr[h, t:t + 1, :] - b_i, 0.0))
                diag[h][i] = jnp.where(hit, jnp.sum(w, axis=-1, keepdims=True), diag[h][i])
    lane_t = lax.broadcasted_iota(jnp.int32, (c, c), 1)
    row_s = lax.broadcasted_iota(jnp.int32, (c, c), 0)
    causal = row_s <= lane_t

    results = []
    for h in heads:
        q, kk, ii, state_t = qs[h], kks[h], iis[h], states_t[h]
        b2 = jnp.concatenate(b2s[h], axis=0)
        scores_t = jnp.where(causal, jnp.concatenate(diag[h], axis=0), 0.0) + outs[h]
        i16 = ii.astype(BF16)
        o = lax.dot_general(scores_t.astype(BF16), i16, (((0,), (0,)), ((), ())),
                            preferred_element_type=F32)
        o = o + lax.dot_general((q * jnp.exp2(b2)).astype(BF16), state_t.astype(BF16),
                                (((1,), (1,)), ((), ())), preferred_element_type=F32)
        b_last = b2[c - 1:c, :]
        kd = (kk * jnp.exp2(b_last - b2)).astype(BF16)
        new_state_t = jnp.exp2(b_last) * state_t + lax.dot_general(
            i16, kd, (((0,), (0,)), ((), ())), preferred_element_type=F32)
        results.append((o, new_state_t))
    return results


def _hgrn_scores_t(q, kk, b2_blocks):
    c = HGRN_CHUNK
    sub = SUBLANES
    nb = c // sub
    row = lax.broadcasted_iota(jnp.int32, (c, HGRN_HEAD_DIM), 0)
    b = jnp.concatenate(b2_blocks, axis=0)
    blk = row // sub
    firsts = [blk_b[0:1, :] for blk_b in b2_blocks]
    r_full = jnp.concatenate([jnp.broadcast_to(r, (sub, HGRN_HEAD_DIM)) for r in firsts], axis=0)
    qr = q * jnp.exp2(jnp.minimum(b - r_full, 0.0))
    k_cols, q_cols = [], []
    for i in range(1, nb):
        n = i * sub
        ki = kk[:n] * jnp.exp2(jnp.minimum(firsts[i] - b[:n], 0.0))
        ki = jnp.concatenate([ki, jnp.zeros((c - n, HGRN_HEAD_DIM), F32)], axis=0)
        k_cols.append(ki.astype(BF16))
        q_cols.append(jnp.where(blk == i, qr, 0.0).astype(BF16))
    return lax.dot_general(jnp.concatenate(k_cols, axis=1), jnp.concatenate(q_cols, axis=1),
                           (((1,), (1,)), ((), ())), preferred_element_type=F32)


HGRN_HEADS_PER_STEP = 16


def _hgrn_body(q_ref, k_ref, lf_ref, i_ref, o_ref, state_ref, b_scr, q_scr, *, rows):
    n = pl.program_id(2)

    @pl.when(n == 0)
    def _():
        state_ref[...] = jnp.zeros_like(state_ref)

    def chunk_body(ci, carry):
        r0 = pl.multiple_of(ci * HGRN_CHUNK, HGRN_CHUNK)
        sl = pl.ds(r0, HGRN_CHUNK)
        cols = [pl.ds(hh * HGRN_HEAD_DIM, HGRN_HEAD_DIM) for hh in range(HGRN_HEADS_PER_STEP)]
        results = _hgrn_chunks([q_ref[0, sl, cl] for cl in cols], [k_ref[0, sl, cl] for cl in cols],
                               [lf_ref[0, sl, cl] for cl in cols], [i_ref[0, sl, cl] for cl in cols],
                               [state_ref[hh] for hh in range(HGRN_HEADS_PER_STEP)],
                               b_scr, q_scr)
        for hh, (o, new_state) in enumerate(results):
            state_ref[hh] = new_state
            o_ref[0, sl, cols[hh]] = o
        return carry

    lax.fori_loop(0, rows // HGRN_CHUNK, chunk_body, 0)


def _hgrn_core(q, k, lf, ii, *, rows=512):
    bsz, s, _ = q.shape
    width = HGRN_HEADS_PER_STEP * HGRN_HEAD_DIM
    blk = pl.BlockSpec((1, rows, width), lambda b, h, n: (b, n, h))
    return pl.pallas_call(
        functools.partial(_hgrn_body, rows=rows),
        grid=(bsz, HGRN_HEADS // HGRN_HEADS_PER_STEP, s // rows),
        in_specs=[blk, blk, blk, blk], out_specs=blk,
        out_shape=jax.ShapeDtypeStruct((bsz, s, HGRN_HEADS * HGRN_HEAD_DIM), F32),
        scratch_shapes=[pltpu.VMEM((HGRN_HEADS_PER_STEP, HGRN_HEAD_DIM, HGRN_HEAD_DIM), F32),
                        pltpu.VMEM((HGRN_HEADS_PER_STEP, HGRN_CHUNK, HGRN_HEAD_DIM), F32),
                        pltpu.VMEM((HGRN_HEADS_PER_STEP, HGRN_CHUNK, HGRN_HEAD_DIM), F32)],
        name="hgrn_core", compiler_params=_params(("parallel", "parallel", "arbitrary")),
    )(q, k, lf, ii)


def _hgrn_gate_body(o_ref, g_ref, ng_ref, out_ref):
    o = o_ref[...]
    o = o * lax.rsqrt(jnp.mean(o * o, axis=-1, keepdims=True) + RMS_EPS)
    out_ref[...] = ((o * ng_ref[...]) * g_ref[...]).astype(out_ref.dtype)


def _hgrn_gate(o, g, norm_g, *, tm=512):
    m, d = o.shape
    row = pl.BlockSpec((tm, d), lambda i: (i, 0))
    return pl.pallas_call(
        _hgrn_gate_body, grid=(m // tm,),
        in_specs=[row, row, pl.BlockSpec((1, d), lambda i: (0, 0))], out_specs=row,
        out_shape=jax.ShapeDtypeStruct((m, d), BF16),
        name="hgrn_gate", compiler_params=_params(("parallel",)),
    )(o, g, norm_g.reshape(1, d))


TM = 1024
TN = 1024
TN_NARROW = 512
TK_DEEP = 2048


def _conformer_layer(hidden, h16, bsz, j, w_in, b_in, w_dw, b_dw, ln_g, ln_b, w_out, b_out):
    m, d = h16.shape
    b_in = b_in.reshape(1, 2 * d)
    tn_glu = TN_NARROW // 2
    u = _matmul(h16, w_in, j, tm=TM, tn=tn_glu, tk=d, n_out_cols=d,
                out_dtypes=[F32], epilogue=_ep_glu, name="conv_in_glu",
                w_block_offsets=(0, d // tn_glu),
                extras=[(b_in[:, :d], "row"), (b_in[:, d:], "row")])[0]
    u = _dwconv_ln_silu(u.reshape(bsz, m // bsz, d), w_dw, b_dw, ln_g, ln_b).reshape(m, d)
    return _residual_matmul(u, w_out, j, hidden, bias=b_out, tm=TM, tn=TN_NARROW, tk=d,
                            name="conv_out_res")


def _retention_layer(hidden, h16, bsz, j, cos, sin, w_q, w_k, w_v, w_g, w_o):
    m, d = h16.shape
    s = m // bsz
    rot = [(cos, "side"), (sin, "side")]
    q = _matmul(h16, w_q, j, tm=TM, tn=TN_NARROW, tk=d, n_out_cols=d, out_dtypes=[F32],
                epilogue=functools.partial(_ep_rotary, scale=1.0), name="ret_q", extras=rot)[0]
    k = _matmul(h16, w_k, j, tm=TM, tn=TN_NARROW, tk=d, n_out_cols=d, out_dtypes=[F32],
                epilogue=functools.partial(_ep_rotary, scale=RET_QK_DIM ** -0.5), name="ret_k",
                extras=rot)[0]
    dv = RET_HEADS * RET_V_DIM
    v = _matmul(h16, w_v, j, tm=TM, tn=TN_NARROW, tk=d, n_out_cols=dv, out_dtypes=[BF16],
                epilogue=_ep_plain, name="ret_v")[0]
    g = _matmul(h16, w_g, j, tm=TM, tn=TN_NARROW, tk=d, n_out_cols=dv, out_dtypes=[F32],
                epilogue=_ep_silu, name="ret_g")[0]
    o = _retention_core(q.reshape(bsz, s, d), k.reshape(bsz, s, d), v.reshape(bsz, s, dv),
                        g.reshape(bsz, s, dv)).reshape(m, dv)
    return _residual_matmul(o, w_o, j, hidden, tm=TM, tn=TN, tk=TK_DEEP, name="ret_out_res")


def _hgrn_layer(hidden, h16, bsz, layer, j, lower_bounds, w_q, w_f, w_i, w_g, norm_g, w_o):
    m, d = h16.shape
    s = m // bsz
    q = _matmul(h16, w_q, j, tm=TM, tn=TN_NARROW, tk=d, n_out_cols=d, out_dtypes=[F32],
                epilogue=_ep_silu, name="hgrn_q")[0]
    log_f, k = _matmul(h16, w_f, j, tm=TM, tn=TN_NARROW, tk=d, n_out_cols=d,
                       out_dtypes=[F32, F32],
                       epilogue=functools.partial(_ep_hgrn_forget, layer=layer), name="hgrn_f",
                       extras=[(lower_bounds, "rows")])
    ii = _matmul(h16, w_i, j, tm=TM, tn=TN_NARROW, tk=d, n_out_cols=d, out_dtypes=[BF16],
                 epilogue=_ep_plain, name="hgrn_i")[0]
    g = _matmul(h16, w_g, j, tm=TM, tn=TN_NARROW, tk=d, n_out_cols=d, out_dtypes=[F32],
                epilogue=_ep_silu, name="hgrn_g")[0]
    shp = (bsz, s, d)
    o = _hgrn_core(q.reshape(shp), k.reshape(shp), log_f.reshape(shp), ii.reshape(shp)).reshape(m, d)
    o = _hgrn_gate(o, g, norm_g)
    return _residual_matmul(o, w_o, j, hidden, tm=TM, tn=TN_NARROW, tk=d, name="hgrn_out_res")


def _mlp_layer(hidden, h16, i, w1, w2):
    m, d = h16.shape
    a = _matmul(h16, w1, i, tm=TM, tn=TN_NARROW, tk=d, n_out_cols=w1.shape[2], out_dtypes=[BF16],
                epilogue=_ep_relu2, name="mlp_up")[0]
    return _residual_matmul(a, w2, i, hidden, tm=TM, tn=TN, tk=TK_DEEP, name="mlp_down_res")


def kernel(x, positions, conv_w_in, conv_b_in, conv_w_dw, conv_b_dw, conv_ln_g, conv_ln_b,
           conv_w_out, conv_b_out, ret_w_q, ret_w_k, ret_w_v, ret_w_g, ret_w_o,
           hgrn_lower_bounds, hgrn_w_q, hgrn_w_f, hgrn_w_i, hgrn_w_g, hgrn_norm_g, hgrn_w_o,
           mlp_w1, mlp_w2, ln_mix_g, ln_mix_b, ln_mlp_g, ln_mlp_b):
    bsz, s, d = x.shape
    m = bsz * s
    x2 = x.reshape(m, d)
    hidden = ("direct", x2)
    h16 = x2.astype(BF16)
    cos, sin = _rope_tables(positions)
    for i in range(DEPTH):
        kind, j = i % N_MIXERS, i // N_MIXERS
        if kind == 0:
            z = _conformer_layer(hidden, h16, bsz, j, conv_w_in, conv_b_in[j], conv_w_dw[j],
                                 conv_b_dw[j], conv_ln_g[j], conv_ln_b[j], conv_w_out,
                                 conv_b_out[j])
        elif kind == 1:
            z = _retention_layer(hidden, h16, bsz, j, cos, sin, ret_w_q, ret_w_k, ret_w_v,
                                 ret_w_g, ret_w_o)
        else:
            z = _hgrn_layer(hidden, h16, bsz, i, j, hgrn_lower_bounds, hgrn_w_q, hgrn_w_f,
                            hgrn_w_i, hgrn_w_g, hgrn_norm_g[j], hgrn_w_o)
        h16, stats = _layer_norm(z, ln_mix_g[i], ln_mix_b[i])
        hidden = ("ln", z, stats, ln_mix_g[i], ln_mix_b[i])
        z = _mlp_layer(hidden, h16, i, mlp_w1, mlp_w2)
        if i == DEPTH - 1:
            return _layer_norm(z, ln_mlp_g[i], ln_mlp_b[i], final=True).reshape(bsz, s, d)
        h16, stats = _layer_norm(z, ln_mlp_g[i], ln_mlp_b[i])
        hidden = ("ln", z, stats, ln_mlp_g[i], ln_mlp_b[i])
```
